```python
import jax, jax.numpy as jnp
from jax import lax
import numpy as np


D_MODEL = 1024
BATCH = 8
SEQ = 2048
DEPTH = 2
DEC_BATCH = 4
DEC_SEQ = 4096
PAST_LEN = 128

N_MIXERS = 2
N_ATTN_LAYERS = (DEPTH + 1) // 2
N_RET_LAYERS = DEPTH // 2
HEAD_DIM = 64
N_HEADS = D_MODEL // HEAD_DIM
N_KV_HEADS = 4
GROUP = N_HEADS // N_KV_HEADS
WINDOW = 128
BLOCK = 128
ROT_DIM = HEAD_DIM // 4
ROPE_THETA = 500000.0
Q_DIM = N_HEADS * HEAD_DIM
KV_DIM = N_KV_HEADS * HEAD_DIM
QKV_DIM = Q_DIM + 2 * KV_DIM
RET_HEADS = 4
RET_QK_DIM = D_MODEL // RET_HEADS
RET_V_TOTAL = 2 * D_MODEL
RET_V_DIM = RET_V_TOTAL // RET_HEADS
RET_IN_DIM = 2 * D_MODEL + 2 * RET_V_TOTAL
RET_THETA = 10000.0
CHUNK = 128
D_FF = 2816
FFN_RESIDUAL = 0.5
NORM_EPS = 1e-6

kernel_name = 'hybrid_bidir_swa_retention_macaron'


def rmsnorm(x, g):
    x32 = x.astype(jnp.float32)
    y = x32 * lax.rsqrt(jnp.mean(x32 * x32, axis=-1, keepdims=True) + NORM_EPS)
    return (y * g.astype(jnp.float32)).astype(x.dtype)


def rotary(x, n_rot, theta):
    L = x.shape[1]
    half = n_rot // 2
    inv_freq = theta ** (-jnp.arange(half, dtype=jnp.float32) / half)
    ang = jnp.arange(L, dtype=jnp.float32)[:, None] * inv_freq[None, :]
    cos = jnp.cos(ang)[None, :, None, :]
    sin = jnp.sin(ang)[None, :, None, :]
    xr = x[..., :n_rot].astype(jnp.float32)
    x1, x2 = xr[..., :half], xr[..., half:]
    rot = jnp.concatenate([x1 * cos - x2 * sin, x2 * cos + x1 * sin], axis=-1)
    return jnp.concatenate([rot.astype(x.dtype), x[..., n_rot:]], axis=-1)


def swiglu(h, w_in, w_out):
    gate, up = jnp.split(h @ w_in, 2, axis=-1)
    return (jax.nn.silu(gate) * up) @ w_out


def windowed_gqa(h, w_qkv, w_o, sink):
    B, L, _ = h.shape
    nb = L // BLOCK
    qkv = h @ w_qkv
    q = qkv[..., :Q_DIM].reshape(B, L, N_HEADS, HEAD_DIM)
    k = qkv[..., Q_DIM:Q_DIM + KV_DIM].reshape(B, L, N_KV_HEADS, HEAD_DIM)
    v = qkv[..., Q_DIM + KV_DIM:].reshape(B, L, N_KV_HEADS, HEAD_DIM)
    q = rotary(q, ROT_DIM, ROPE_THETA)
    k = rotary(k, ROT_DIM, ROPE_THETA)
    qb = q.reshape(B, nb, BLOCK, N_KV_HEADS, GROUP, HEAD_DIM)
    pad = ((0, 0), (BLOCK, BLOCK), (0, 0), (0, 0))
    kp = jnp.pad(k, pad).reshape(B, nb + 2, BLOCK, N_KV_HEADS, HEAD_DIM)
    vp = jnp.pad(v, pad).reshape(B, nb + 2, BLOCK, N_KV_HEADS, HEAD_DIM)
    kb = jnp.concatenate([kp[:, :-2], kp[:, 1:-1], kp[:, 2:]], axis=2)
    vb = jnp.concatenate([vp[:, :-2], vp[:, 1:-1], vp[:, 2:]], axis=2)
    s = jnp.einsum('bnqgrd,bnkgd->bngrqk', qb, kb).astype(jnp.float32) * (HEAD_DIM ** -0.5)
    qi = jnp.arange(BLOCK)
    kj = jnp.arange(3 * BLOCK)
    rel = kj[None, :] - BLOCK - qi[:, None]
    kpos = jnp.arange(nb)[:, None] * BLOCK + kj[None, :] - BLOCK
    valid = (jnp.abs(rel) <= WINDOW)[None] & ((kpos >= 0) & (kpos < L))[:, None, :]
    s = jnp.where(valid[None, :, None, None], s, -jnp.inf)
    sink_l = sink.astype(jnp.float32).reshape(N_KV_HEADS, GROUP)[None, None, :, :, None, None]
    m = jnp.maximum(jnp.max(s, axis=-1, keepdims=True), sink_l)
    p = jnp.exp(s - m)
    denom = jnp.sum(p, axis=-1, keepdims=True) + jnp.exp(sink_l - m)
    p = (p / denom).astype(vb.dtype)
    o = jnp.einsum('bngrqk,bnkgd->bnqgrd', p, vb).reshape(B, L, Q_DIM)
    return o @ w_o


def retention_direction(q, k, v, log_gamma, strict):
    B, L, H, dk = q.shape
    dv = v.shape[-1]
    nc = L // CHUNK
    i = jnp.arange(CHUNK, dtype=jnp.float32)
    diff = i[:, None] - i[None, :]
    mask = (diff > 0) if strict else (diff >= 0)
    decay = jnp.where(mask[None], jnp.exp(log_gamma[:, None, None] * jnp.maximum(diff, 0.0)[None]), 0.0)
    qc = q.reshape(B, nc, CHUNK, H, dk)
    kc = k.reshape(B, nc, CHUNK, H, dk)
    vc = v.reshape(B, nc, CHUNK, H, dv)
    s = jnp.einsum('bnihd,bnjhd->bnhij', qc, kc) * decay
    intra = jnp.einsum('bnhij,bnjhe->bnihe', s, vc)
    q_decay = jnp.exp(log_gamma[None, :] * (i[:, None] + 1.0))
    k_decay = jnp.exp(log_gamma[None, :] * (CHUNK - 1.0 - i)[:, None])
    chunk_decay = jnp.exp(log_gamma * CHUNK)

    def step(state, xs):
        qn, kn, vn = xs
        cross = jnp.einsum('bihd,bhde->bihe', qn, state) * q_decay[None, :, :, None]
        state = chunk_decay[None, :, None, None] * state + jnp.einsum(
            'bjhd,bjhe->bhde', kn * k_decay[None, :, :, None], vn)
        return state, cross

    state0 = jnp.zeros((B, H, dk, dv), jnp.float32)
    _, cross = lax.scan(step, state0, (jnp.moveaxis(qc, 1, 0), jnp.moveaxis(kc, 1, 0), jnp.moveaxis(vc, 1, 0)))
    out = intra + jnp.moveaxis(cross, 0, 1)
    return out.reshape(B, L, H, dv)


def retention(h, w_in, w_o, decay_fwd, decay_bwd):
    B, L, _ = h.shape
    proj = h @ w_in
    q = proj[..., :D_MODEL].reshape(B, L, RET_HEADS, RET_QK_DIM)
    k = proj[..., D_MODEL:2 * D_MODEL].reshape(B, L, RET_HEADS, RET_QK_DIM)
    v = proj[..., 2 * D_MODEL:2 * D_MODEL + RET_V_TOTAL].reshape(B, L, RET_HEADS, RET_V_DIM)
    g = proj[..., 2 * D_MODEL + RET_V_TOTAL:]
    q = rotary(q, RET_QK_DIM, RET_THETA).astype(jnp.float32)
    k = rotary(k, RET_QK_DIM, RET_THETA).astype(jnp.float32) * (RET_QK_DIM ** -0.5)
    v = v.astype(jnp.float32)
    lg_f = jax.nn.log_sigmoid(decay_fwd.astype(jnp.float32))
    lg_b = jax.nn.log_sigmoid(decay_bwd.astype(jnp.float32))
    y_f = retention_direction(q, k, v, lg_f, False)
    y_b = jnp.flip(retention_direction(jnp.flip(q, 1), jnp.flip(k, 1), jnp.flip(v, 1), lg_b, True), 1)
    y = y_f + y_b
    y = y * lax.rsqrt(jnp.mean(y * y, axis=-1, keepdims=True) + NORM_EPS)
    y = y.reshape(B, L, RET_V_TOTAL).astype(h.dtype)
    return (jax.nn.silu(g) * y) @ w_o


def trunk(x, norm_gains, ffn_w_in, ffn_w_out, attn_w_qkv, attn_w_o, attn_sink,
          ret_w_in, ret_w_o, ret_decay_fwd, ret_decay_bwd):
    for l in range(DEPTH):
        g = norm_gains[l]
        x = x + FFN_RESIDUAL * rmsnorm(swiglu(rmsnorm(x, g[0]), ffn_w_in[l, 0], ffn_w_out[l, 0]), g[1])
        h = rmsnorm(x, g[2])
        if l % N_MIXERS == 0:
            a = l // N_MIXERS
            mix = windowed_gqa(h, attn_w_qkv[a], attn_w_o[a], attn_sink[a])
        else:
            r = l // N_MIXERS
            mix = retention(h, ret_w_in[r], ret_w_o[r], ret_decay_fwd[r], ret_decay_bwd[r])
        x = x + rmsnorm(mix, g[3])
        x = x + FFN_RESIDUAL * rmsnorm(swiglu(rmsnorm(x, g[4]), ffn_w_in[l, 1], ffn_w_out[l, 1]), g[5])
    return x


def setup_inputs(seed: int = 0) -> dict:
    key = jax.random.key(seed)
    ks = jax.random.split(key, 14)
    f32 = jnp.float32
    base = 1.0 - 2.0 ** (-5.0 - np.arange(RET_HEADS))
    decay_logit = jnp.asarray(np.log(base / (1.0 - base)).astype(np.float32))[None, :]
    return {
        'x_prompt': jax.random.normal(ks[0], (BATCH, SEQ, D_MODEL), f32),
        'x_sample': jax.random.normal(ks[1], (DEC_BATCH, DEC_SEQ, D_MODEL), f32),
        'norm_gains': 1.0 + 0.05 * jax.random.normal(ks[2], (DEPTH, 6, D_MODEL), f32),
        'ffn_w_in': jax.random.normal(ks[3], (DEPTH, 2, D_MODEL, 2 * D_FF), f32) * D_MODEL ** -0.5,
        'ffn_w_out': jax.random.normal(ks[4], (DEPTH, 2, D_FF, D_MODEL), f32) * D_FF ** -0.5,
        'attn_w_qkv': jax.random.normal(ks[5], (N_ATTN_LAYERS, D_MODEL, QKV_DIM), f32) * D_MODEL ** -0.5,
        'attn_w_o': jax.random.normal(ks[6], (N_ATTN_LAYERS, Q_DIM, D_MODEL), f32) * Q_DIM ** -0.5,
        'attn_sink': 0.5 * jax.random.normal(ks[7], (N_ATTN_LAYERS, N_HEADS), f32),
        'ret_w_in': jax.random.normal(ks[8], (N_RET_LAYERS, D_MODEL, RET_IN_DIM), f32) * D_MODEL ** -0.5,
        'ret_w_o': jax.random.normal(ks[9], (N_RET_LAYERS, RET_V_TOTAL, D_MODEL), f32) * RET_V_TOTAL ** -0.5,
        'ret_decay_fwd': decay_logit + 0.01 * jax.random.normal(ks[10], (N_RET_LAYERS, RET_HEADS), f32),
        'ret_decay_bwd': decay_logit + 0.01 * jax.random.normal(ks[11], (N_RET_LAYERS, RET_HEADS), f32),
    }


def reference(x_prompt, x_sample, norm_gains, ffn_w_in, ffn_w_out, attn_w_qkv, attn_w_o, attn_sink,
              ret_w_in, ret_w_o, ret_decay_fwd, ret_decay_bwd):
    y_prompt = trunk(x_prompt, norm_gains, ffn_w_in, ffn_w_out, attn_w_qkv, attn_w_o, attn_sink,
                     ret_w_in, ret_w_o, ret_decay_fwd, ret_decay_bwd)
    y_sample = trunk(x_sample, norm_gains, ffn_w_in, ffn_w_out, attn_w_qkv, attn_w_o, attn_sink,
                     ret_w_in, ret_w_o, ret_decay_fwd, ret_decay_bwd)
    return (y_prompt, y_sample)
```

```python
import functools

import jax
import jax.numpy as jnp
import numpy as np
from jax import lax
from jax.experimental import pallas as pl
from jax.experimental.pallas import tpu as pltpu

D_MODEL = 1024
HEAD_DIM = 64
N_HEADS = 16
N_KV_HEADS = 4
GROUP = 4
ATT_BLOCK = 128
ROT_DIM = 16
ROPE_THETA = 500000.0
RET_HEADS = 4
RET_QK_DIM = 256
RET_V_DIM = 512
RET_V_TOTAL = 2048
RET_IN_DIM = 6144
RET_THETA = 10000.0
CHUNK = 128
D_FF = 2816
NORM_EPS = 1e-6

LANES = 128
FF_TILE = 256
N_FF_TILES = D_FF // FF_TILE
TOKEN_TILE = 512
ATT_Q_TILE = 512
VMEM_LIMIT = 56 * 1024 * 1024

F32 = jnp.float32
BF16 = jnp.bfloat16


def _rms(x, g):
    return x * lax.rsqrt(jnp.mean(x * x, axis=-1, keepdims=True) + NORM_EPS) * g


def _const_spec(shape):
    zeros = (0,) * len(shape)
    return pl.BlockSpec(shape, lambda *_: zeros, pipeline_mode=pl.Buffered(1))


def _params(sem):
    return pltpu.CompilerParams(dimension_semantics=sem, vmem_limit_bytes=VMEM_LIMIT)


def _ffn_kernel(x_ref, gpre_ref, gpost_ref, wg_ref, wu_ref, wo_ref, o_ref, acc_ref):
    x = x_ref[...]
    h = _rms(x, gpre_ref[...]).astype(BF16)

    def step(f, carry):
        gate = jnp.dot(h, wg_ref[f], preferred_element_type=F32)
        up = jnp.dot(h, wu_ref[f], preferred_element_type=F32)
        act = (gate * jax.nn.sigmoid(gate) * up).astype(BF16)
        contrib = jnp.dot(act, wo_ref[f], preferred_element_type=F32)

        @pl.when(f == 0)
        def _():
            acc_ref[...] = contrib

        @pl.when(f > 0)
        def _():
            acc_ref[...] += contrib

        return carry

    lax.fori_loop(0, N_FF_TILES, step, 0)
    o_ref[...] = x + 0.5 * _rms(acc_ref[...], gpost_ref[...])


def _ffn(x, g_pre, g_post, wg, wu, wo):
    n = x.shape[0]
    tm = TOKEN_TILE
    row = pl.BlockSpec((tm, D_MODEL), lambda i: (i, 0))
    return pl.pallas_call(
        _ffn_kernel,
        grid=(n // tm,),
        in_specs=[row, _const_spec((1, D_MODEL)), _const_spec((1, D_MODEL)),
                  _const_spec(wg.shape), _const_spec(wu.shape), _const_spec(wo.shape)],
        out_specs=row,
        out_shape=jax.ShapeDtypeStruct((n, D_MODEL), F32),
        scratch_shapes=[pltpu.VMEM((tm, D_MODEL), F32)],
        compiler_params=_params(("parallel",)),
        name="ffn",
    )(x, g_pre, g_post, wg, wu, wo)


def _attn_proj_kernel(x_ref, g_ref, w_ref, c_ref, s1_ref, s2_ref, q_ref, k_ref, v_ref):
    h = _rms(x_ref[...], g_ref[...]).astype(BF16)
    qkv = jnp.dot(h, w_ref[...], preferred_element_type=F32)
    c, s1, s2 = c_ref[...], s1_ref[...], s2_ref[...]
    n_q = N_HEADS * HEAD_DIM // LANES
    n_k = N_KV_HEADS
    for j in range(n_q + n_k):
        xj = qkv[:, j * LANES:(j + 1) * LANES]
        rot = xj * c + pltpu.roll(xj, LANES - ROT_DIM // 2, 1) * s1 + pltpu.roll(xj, ROT_DIM // 2, 1) * s2
        if j < n_q:
            q_ref[:, j * LANES:(j + 1) * LANES] = (rot * (HEAD_DIM ** -0.5)).astype(BF16)
        else:
            jk = j - n_q
            k_ref[:, jk * LANES:(jk + 1) * LANES] = rot.astype(BF16)
    v_ref[...] = qkv[:, (n_q + n_k) * LANES:].astype(BF16)


def _attn_proj(x, g, w, tabs, seq_len):
    n = x.shape[0]
    tm = TOKEN_TILE
    per_seq = seq_len // tm
    row = pl.BlockSpec((tm, D_MODEL), lambda i: (i, 0))
    tab = pl.BlockSpec((tm, LANES), lambda i: (i % per_seq, 0))
    kv_w = N_KV_HEADS * LANES
    return pl.pallas_call(
        _attn_proj_kernel,
        grid=(n // tm,),
        in_specs=[row, _const_spec((1, D_MODEL)), _const_spec(w.shape), tab, tab, tab],
        out_specs=[row, pl.BlockSpec((tm, kv_w), lambda i: (i, 0)), pl.BlockSpec((tm, kv_w), lambda i: (i, 0))],
        out_shape=[jax.ShapeDtypeStruct((n, D_MODEL), BF16), jax.ShapeDtypeStruct((n, kv_w), BF16),
                   jax.ShapeDtypeStruct((n, kv_w), BF16)],
        compiler_params=_params(("parallel",)),
        name="attn_proj",
    )(x, g, w, *tabs)


def _attn_core_kernel(sink_ref, q_ref, kp_ref, km_ref, kn_ref, vp_ref, vm_ref, vn_ref, x_ref, g_ref, wo_ref,
                      o_ref, kext_ref, vext_ref, obuf_ref, *, n_blocks):
    tq = q_ref.shape[0]
    blocks = tq // ATT_BLOCK
    i = pl.program_id(1)
    blk = ATT_BLOCK
    kext_ref[0:blk] = kp_ref[...]
    kext_ref[blk:blk + tq] = km_ref[...]
    kext_ref[blk + tq:] = kn_ref[...]
    vext_ref[0:blk] = vp_ref[...]
    vext_ref[blk:blk + tq] = vm_ref[...]
    vext_ref[blk + tq:] = vn_ref[...]

    rows = GROUP * blk
    qpos = lax.broadcasted_iota(jnp.int32, (rows, 3 * blk), 0) & (blk - 1)
    col = lax.broadcasted_iota(jnp.int32, (rows, 3 * blk), 1)
    neg_inf = jnp.float32(-jnp.inf)
    band = jnp.where(jnp.abs(col - blk - qpos) <= blk, 0.0, neg_inf)
    col_row = lax.broadcasted_iota(jnp.int32, (1, 3 * blk), 1)
    lane = lax.broadcasted_iota(jnp.int32, (blk, LANES), 1)
    lo = lane < HEAD_DIM

    for qb in range(blocks):
        gb = i * blocks + qb
        edge_lo = jnp.where(gb == 0, blk, 0)
        edge_hi = jnp.where(gb == n_blocks - 1, 2 * blk, 3 * blk)
        edge = jnp.where((col_row >= edge_lo) & (col_row < edge_hi), 0.0, neg_inf)
        bias = band + edge
        r0 = qb * blk
        for g in range(N_KV_HEADS):
            kg = kext_ref[r0:r0 + 3 * blk, g * LANES:(g + 1) * LANES]
            vg = vext_ref[r0:r0 + 3 * blk, g * LANES:(g + 1) * LANES]
            parts = []
            for pair in range(GROUP // 2):
                c0 = (g * GROUP // 2 + pair) * LANES
                qp = q_ref[r0:r0 + blk, c0:c0 + LANES]
                zero = jnp.zeros_like(qp)
                parts.append(jnp.where(lo, qp, zero))
                parts.append(jnp.where(lo, zero, qp))
            qs = jnp.concatenate(parts, axis=0)
            s = lax.dot_general(qs, kg, (((1,), (1,)), ((), ())), preferred_element_type=F32) + bias
            ps, inv = [], []
            for r in range(GROUP):
                sr = s[r * blk:(r + 1) * blk]
                sink = sink_ref[g * GROUP + r]
                m = jnp.maximum(jnp.max(sr, axis=-1, keepdims=True), sink)
                p = jnp.exp(sr - m)
                denom = jnp.sum(p, axis=-1, keepdims=True) + jnp.exp(sink - m)
                ps.append(p.astype(BF16))
                inv.append(1.0 / denom)
            pv = jnp.dot(jnp.concatenate(ps, axis=0), vg, preferred_element_type=F32)
            for pair in range(GROUP // 2):
                a = pv[(2 * pair) * blk:(2 * pair + 1) * blk] * inv[2 * pair]
                b = pv[(2 * pair + 1) * blk:(2 * pair + 2) * blk] * inv[2 * pair + 1]
                c0 = (g * GROUP // 2 + pair) * LANES
                obuf_ref[r0:r0 + blk, c0:c0 + LANES] = jnp.where(lo, a, b).astype(BF16)

    y = jnp.dot(obuf_ref[...], wo_ref[...], preferred_element_type=F32)
    o_ref[...] = x_ref[...] + _rms(y, g_ref[...])


def _attn_core(sink, q, kd, vd, x, g, wo, batch, seq_len):
    tq = ATT_Q_TILE
    blocks = tq // ATT_BLOCK
    n_blocks = seq_len // ATT_BLOCK
    kv_w = N_KV_HEADS * LANES
    q3 = q.reshape(batch, seq_len, D_MODEL)
    k3 = kd.reshape(batch, seq_len, kv_w)
    v3 = vd.reshape(batch, seq_len, kv_w)
    x3 = x.reshape(batch, seq_len, D_MODEL)
    row = pl.BlockSpec((None, tq, D_MODEL), lambda b, i: (b, i, 0))
    kv_main = pl.BlockSpec((None, tq, kv_w), lambda b, i: (b, i, 0))
    kv_prev = pl.BlockSpec((None, ATT_BLOCK, kv_w), lambda b, i: (b, jnp.maximum(i * blocks - 1, 0), 0))
    kv_next = pl.BlockSpec((None, ATT_BLOCK, kv_w), lambda b, i: (b, jnp.minimum((i + 1) * blocks, n_blocks - 1), 0))
    out = pl.pallas_call(
        functools.partial(_attn_core_kernel, n_blocks=n_blocks),
        grid=(batch, seq_len // tq),
        in_specs=[pl.BlockSpec(memory_space=pltpu.SMEM), row, kv_prev, kv_main, kv_next, kv_prev, kv_main, kv_next,
                  row, _const_spec((1, D_MODEL)), _const_spec(wo.shape)],
        out_specs=row,
        out_shape=jax.ShapeDtypeStruct((batch, seq_len, D_MODEL), F32),
        scratch_shapes=[pltpu.VMEM((tq + 2 * ATT_BLOCK, kv_w), BF16), pltpu.VMEM((tq + 2 * ATT_BLOCK, kv_w), BF16),
                        pltpu.VMEM((tq, D_MODEL), BF16)],
        compiler_params=_params(("parallel", "parallel")),
        name="attn_core",
    )(sink, q3, k3, k3, k3, v3, v3, v3, x3, g, wo)
    return out.reshape(batch * seq_len, D_MODEL)


def _ret_proj_kernel(x_ref, g_ref, w_ref, cos_ref, sin_ref, o_ref):
    h = _rms(x_ref[...], g_ref[...]).astype(BF16)
    cos, sin = cos_ref[...], sin_ref[...]
    half = RET_QK_DIM // 2
    n_chunks = RET_IN_DIM // D_MODEL
    for c in range(n_chunks):
        y = jnp.dot(h, w_ref[:, c * D_MODEL:(c + 1) * D_MODEL], preferred_element_type=F32)
        if c < 2:
            scale = 1.0 if c == 0 else RET_QK_DIM ** -0.5
            for hd in range(RET_HEADS):
                x1 = y[:, hd * RET_QK_DIM:hd * RET_QK_DIM + half]
                x2 = y[:, hd * RET_QK_DIM + half:(hd + 1) * RET_QK_DIM]
                base = c * D_MODEL + hd * RET_QK_DIM
                o_ref[:, base:base + half] = ((x1 * cos - x2 * sin) * scale).astype(BF16)
                o_ref[:, base + half:base + 2 * half] = ((x2 * cos + x1 * sin) * scale).astype(BF16)
        else:
            o_ref[:, c * D_MODEL:(c + 1) * D_MODEL] = y.astype(BF16)


def _ret_proj(x, g, w, tabs, seq_len):
    n = x.shape[0]
    tm = TOKEN_TILE
    per_seq = seq_len // tm
    row = pl.BlockSpec((tm, D_MODEL), lambda i: (i, 0))
    tab = pl.BlockSpec((tm, LANES), lambda i: (i % per_seq, 0))
    return pl.pallas_call(
        _ret_proj_kernel,
        grid=(n // tm,),
        in_specs=[row, _const_spec((1, D_MODEL)), _const_spec(w.shape), tab, tab],
        out_specs=pl.BlockSpec((tm, RET_IN_DIM), lambda i: (i, 0)),
        out_shape=jax.ShapeDtypeStruct((n, RET_IN_DIM), BF16),
        compiler_params=_params(("parallel",)),
        name="ret_proj",
    )(x, g, w, *tabs)


def _ret_core_kernel(lgf_ref, lgb_ref, q_ref, k_ref, v_ref, g_ref, o_ref, acc_ref, st_ref):
    seq_len = q_ref.shape[0]
    n_chunks = seq_len // CHUNK
    hd = pl.program_id(1)
    lgf = lgf_ref[hd]
    lgb = lgb_ref[hd]
    ii = lax.broadcasted_iota(jnp.int32, (CHUNK, CHUNK), 0)
    jj = lax.broadcasted_iota(jnp.int32, (CHUNK, CHUNK), 1)
    diff = (ii - jj).astype(F32)
    dmat = jnp.where(diff >= 0, jnp.exp(lgf * jnp.maximum(diff, 0.0)), jnp.exp(lgb * jnp.maximum(-diff, 0.0)))
    idx = lax.broadcasted_iota(jnp.int32, (CHUNK, 1), 0).astype(F32)
    qdec_f = jnp.exp(lgf * (idx + 1.0))
    kdec_f = jnp.exp(lgf * (CHUNK - 1.0 - idx))
    cdec_f = jnp.exp(lgf * CHUNK)
    qdec_b = jnp.exp(lgb * (CHUNK - idx))
    kdec_b = jnp.exp(lgb * idx)
    cdec_b = jnp.exp(lgb * CHUNK)

    def state_update(kc, vc, kdec, cdec):
        kd = (kc.astype(F32) * kdec).astype(BF16)
        upd = lax.dot_general(kd, vc, (((0,), (0,)), ((), ())), preferred_element_type=F32)
        st_ref[...] = cdec * st_ref[...] + upd

    st_ref[...] = jnp.zeros_like(st_ref)

    def fwd(c, carry):
        rows = pl.ds(pl.multiple_of(c * CHUNK, CHUNK), CHUNK)
        qc, kc, vc = q_ref[rows, :], k_ref[rows, :], v_ref[rows, :]
        s = lax.dot_general(qc, kc, (((1,), (1,)), ((), ())), preferred_element_type=F32) * dmat
        intra = jnp.dot(s.astype(BF16), vc, preferred_element_type=F32)
        cross = jnp.dot(qc, st_ref[...].astype(BF16), preferred_element_type=F32) * qdec_f
        acc_ref[rows, :] = intra + cross
        state_update(kc, vc, kdec_f, cdec_f)
        return carry

    lax.fori_loop(0, n_chunks, fwd, 0)

    st_ref[...] = jnp.zeros_like(st_ref)

    def bwd(t, carry):
        c = n_chunks - 1 - t
        rows = pl.ds(pl.multiple_of(c * CHUNK, CHUNK), CHUNK)
        qc, kc, vc = q_ref[rows, :], k_ref[rows, :], v_ref[rows, :]
        cross = jnp.dot(qc, st_ref[...].astype(BF16), preferred_element_type=F32) * qdec_b
        y = acc_ref[rows, :] + cross
        yn = y * lax.rsqrt(jnp.mean(y * y, axis=-1, keepdims=True) + NORM_EPS)
        gate = g_ref[rows, :].astype(F32)
        o_ref[rows, :] = (gate * jax.nn.sigmoid(gate) * yn).astype(BF16)
        state_update(kc, vc, kdec_b, cdec_b)
        return carry

    lax.fori_loop(0, n_chunks, bwd, 0)


def _ret_core(lgf, lgb, proj, batch, seq_len):
    p3 = proj.reshape(batch, seq_len, RET_IN_DIM)
    qk_blocks = D_MODEL // RET_QK_DIM
    v_off = 2 * D_MODEL // RET_V_DIM
    g_off = v_off + RET_V_TOTAL // RET_V_DIM
    smem = pl.BlockSpec(memory_space=pltpu.SMEM)
    out = pl.pallas_call(
        _ret_core_kernel,
        grid=(batch, RET_HEADS),
        in_specs=[smem, smem,
                  pl.BlockSpec((None, seq_len, RET_QK_DIM), lambda b, h: (b, 0, h)),
                  pl.BlockSpec((None, seq_len, RET_QK_DIM), lambda b, h: (b, 0, qk_blocks + h)),
                  pl.BlockSpec((None, seq_len, RET_V_DIM), lambda b, h: (b, 0, v_off + h)),
                  pl.BlockSpec((None, seq_len, RET_V_DIM), lambda b, h: (b, 0, g_off + h))],
        out_specs=pl.BlockSpec((None, seq_len, RET_V_DIM), lambda b, h: (b, 0, h)),
        out_shape=jax.ShapeDtypeStruct((batch, seq_len, RET_V_TOTAL), BF16),
        scratch_shapes=[pltpu.VMEM((seq_len, RET_V_DIM), F32), pltpu.VMEM((RET_QK_DIM, RET_V_DIM), F32)],
        compiler_params=_params(("parallel", "parallel")),
        name="ret_core",
    )(lgf, lgb, p3, p3, p3, p3)
    return out.reshape(batch * seq_len, RET_V_TOTAL)


def _ret_out_kernel(y_ref, x_ref, g_ref, w_ref, o_ref):
    y = jnp.dot(y_ref[...], w_ref[...], preferred_element_type=F32)
    o_ref[...] = x_ref[...] + _rms(y, g_ref[...])


def _ret_out(y, x, g, w):
    n = x.shape[0]
    tm = TOKEN_TILE
    row = pl.BlockSpec((tm, D_MODEL), lambda i: (i, 0))
    return pl.pallas_call(
        _ret_out_kernel,
        grid=(n // tm,),
        in_specs=[pl.BlockSpec((tm, RET_V_TOTAL), lambda i: (i, 0)), row, _const_spec((1, D_MODEL)),
                  _const_spec(w.shape)],
        out_specs=row,
        out_shape=jax.ShapeDtypeStruct((n, D_MODEL), F32),
        compiler_params=_params(("parallel",)),
        name="ret_out",
    )(y, x, g, w)


def _attn_tables(seq_len):
    half = ROT_DIM // 2
    inv_freq = ROPE_THETA ** (-jnp.arange(half, dtype=F32) / half)
    ang = jnp.arange(seq_len, dtype=F32)[:, None] * inv_freq[None, :]
    cos, sin = jnp.cos(ang), jnp.sin(ang)
    pad = HEAD_DIM - ROT_DIM
    ones = jnp.ones((seq_len, pad), F32)
    zeros = jnp.zeros((seq_len, pad), F32)
    zh = jnp.zeros((seq_len, half), F32)
    c = jnp.concatenate([cos, cos, ones], axis=1)
    s1 = jnp.concatenate([-sin, zh, zeros], axis=1)
    s2 = jnp.concatenate([zh, sin, zeros], axis=1)
    reps = LANES // HEAD_DIM
    return tuple(jnp.tile(t, (1, reps)) for t in (c, s1, s2))


def _ret_tables(seq_len):
    half = RET_QK_DIM // 2
    inv_freq = RET_THETA ** (-jnp.arange(half, dtype=F32) / half)
    ang = jnp.arange(seq_len, dtype=F32)[:, None] * inv_freq[None, :]
    return jnp.cos(ang), jnp.sin(ang)


def _ffn_weights(w_in, w_out):
    wg = w_in[:, :D_FF].reshape(D_MODEL, N_FF_TILES, FF_TILE).transpose(1, 0, 2).astype(BF16)
    wu = w_in[:, D_FF:].reshape(D_MODEL, N_FF_TILES, FF_TILE).transpose(1, 0, 2).astype(BF16)
    wo = w_out.reshape(N_FF_TILES, FF_TILE, D_MODEL).astype(BF16)
    return wg, wu, wo


def _attn_weights(w_qkv):
    q_dim = N_HEADS * HEAD_DIM
    kv_dim = N_KV_HEADS * HEAD_DIM
    wq = w_qkv[:, :q_dim]
    wk = w_qkv[:, q_dim:q_dim + kv_dim].reshape(D_MODEL, N_KV_HEADS, 1, HEAD_DIM)
    wv = w_qkv[:, q_dim + kv_dim:].reshape(D_MODEL, N_KV_HEADS, 1, HEAD_DIM)
    reps = LANES // HEAD_DIM
    wk = jnp.broadcast_to(wk, (D_MODEL, N_KV_HEADS, reps, HEAD_DIM)).reshape(D_MODEL, N_KV_HEADS * LANES)
    wv = jnp.broadcast_to(wv, (D_MODEL, N_KV_HEADS, reps, HEAD_DIM)).reshape(D_MODEL, N_KV_HEADS * LANES)
    return jnp.concatenate([wq, wk, wv], axis=1).astype(BF16)


def _trunk(x, gains, ffn_w, attn_w, attn_wo, sink, ret_w, ret_wo, lgf, lgb, attn_tabs, ret_tabs):
    batch, seq_len, _ = x.shape
    x = x.reshape(batch * seq_len, D_MODEL)
    for layer in range(2):
        g = gains[layer]
        x = _ffn(x, g[0], g[1], *ffn_w[layer][0])
        if layer == 0:
            q, kd, vd = _attn_proj(x, g[2], attn_w, attn_tabs, seq_len)
            x = _attn_core(sink, q, kd, vd, x, g[3], attn_wo, batch, seq_len)
        else:
            proj = _ret_proj(x, g[2], ret_w, ret_tabs, seq_len)
            y = _ret_core(lgf, lgb, proj, batch, seq_len)
            x = _ret_out(y, x, g[3], ret_wo)
        x = _ffn(x, g[4], g[5], *ffn_w[layer][1])
    return x.reshape(batch, seq_len, D_MODEL)


def kernel(x_prompt, x_sample, norm_gains, ffn_w_in, ffn_w_out, attn_w_qkv, attn_w_o, attn_sink, ret_w_in, ret_w_o, ret_decay_fwd, ret_decay_bwd):
    gains = norm_gains.astype(F32).reshape(2, 6, 1, D_MODEL)
    ffn_w = [[_ffn_weights(ffn_w_in[l, s], ffn_w_out[l, s]) for s in range(2)] for l in range(2)]
    attn_w = _attn_weights(attn_w_qkv[0])
    attn_wo = attn_w_o[0].astype(BF16)
    sink = attn_sink[0].astype(F32)
    ret_w = ret_w_in[0].astype(BF16)
    ret_wo = ret_w_o[0].astype(BF16)
    lgf = jax.nn.log_sigmoid(ret_decay_fwd[0].astype(F32))
    lgb = jax.nn.log_sigmoid(ret_decay_bwd[0].astype(F32))
    outs = []
    for x in (x_prompt, x_sample):
        seq_len = x.shape[1]
        outs.append(_trunk(x, gains, ffn_w, attn_w, attn_wo, sink, ret_w, ret_wo, lgf, lgb,
                           _attn_tables(seq_len), _ret_tables(seq_len)))
    return tuple(outs)
```

```python
import functools

import jax
import jax.numpy as jnp
import numpy as np
from jax import lax
from jax.experimental import pallas as pl
from jax.experimental.pallas import tpu as pltpu

D_MODEL = 1024
HEAD_DIM = 64
N_HEADS = 16
N_KV_HEADS = 4
GROUP = 4
ATT_BLOCK = 128
ROT_DIM = 16
ROPE_THETA = 500000.0
RET_HEADS = 4
RET_QK_DIM = 256
RET_V_DIM = 512
RET_V_TOTAL = 2048
RET_IN_DIM = 6144
RET_THETA = 10000.0
CHUNK = 128
D_FF = 2816
NORM_EPS = 1e-6

LANES = 128
FF_TILE = 256
N_FF_TILES = D_FF // FF_TILE
TOKEN_TILE = 512
ATT_Q_TILE = 512
RET_UNROLL = 4
VMEM_LIMIT = 56 * 1024 * 1024

F32 = jnp.float32
BF16 = jnp.bfloat16


def _rms(x, g):
    return x * lax.rsqrt(jnp.mean(x * x, axis=-1, keepdims=True) + NORM_EPS) * g


def _const_spec(shape):
    zeros = (0,) * len(shape)
    return pl.BlockSpec(shape, lambda *_: zeros, pipeline_mode=pl.Buffered(1))


def _params(sem):
    return pltpu.CompilerParams(dimension_semantics=sem, vmem_limit_bytes=VMEM_LIMIT)


def _ffn_kernel(x_ref, gpre_ref, gpost_ref, wi_ref, wo_ref, o_ref, act_ref):
    x = x_ref[...]
    h = _rms(x, gpre_ref[...]).astype(BF16)
    for f in range(N_FF_TILES):
        gate = jnp.dot(h, wi_ref[:, f * FF_TILE:(f + 1) * FF_TILE], preferred_element_type=F32)
        up = jnp.dot(h, wi_ref[:, D_FF + f * FF_TILE:D_FF + (f + 1) * FF_TILE], preferred_element_type=F32)
        act_ref[:, f * FF_TILE:(f + 1) * FF_TILE] = (gate * jax.nn.sigmoid(gate) * up).astype(BF16)
    y = jnp.dot(act_ref[...], wo_ref[...], preferred_element_type=F32)
    o_ref[...] = x + 0.5 * _rms(y, gpost_ref[...])


def _ffn(x, g_pre, g_post, wi, wo):
    n = x.shape[0]
    tm = TOKEN_TILE
    row = pl.BlockSpec((tm, D_MODEL), lambda i: (i, 0))
    return pl.pallas_call(
        _ffn_kernel,
        grid=(n // tm,),
        in_specs=[row, _const_spec((1, D_MODEL)), _const_spec((1, D_MODEL)),
                  _const_spec(wi.shape), _const_spec(wo.shape)],
        out_specs=row,
        out_shape=jax.ShapeDtypeStruct((n, D_MODEL), F32),
        scratch_shapes=[pltpu.VMEM((tm, D_FF), BF16)],
        compiler_params=_params(("parallel",)),
        name="ffn",
    )(x, g_pre, g_post, wi, wo)


def _attn_proj_kernel(x_ref, g_ref, w_ref, c_ref, s1_ref, s2_ref, q_ref, k_ref, v_ref):
    h = _rms(x_ref[...], g_ref[...]).astype(BF16)
    qkv = jnp.dot(h, w_ref[...], preferred_element_type=F32)
    c, s1, s2 = c_ref[...], s1_ref[...], s2_ref[...]
    n_q = N_HEADS * HEAD_DIM // LANES
    n_k = N_KV_HEADS
    for j in range(n_q + n_k):
        xj = qkv[:, j * LANES:(j + 1) * LANES]
        rot = xj * c + pltpu.roll(xj, LANES - ROT_DIM // 2, 1) * s1 + pltpu.roll(xj, ROT_DIM // 2, 1) * s2
        if j < n_q:
            q_ref[:, j * LANES:(j + 1) * LANES] = (rot * (HEAD_DIM ** -0.5)).astype(BF16)
        else:
            jk = j - n_q
            k_ref[:, jk * LANES:(jk + 1) * LANES] = rot.astype(BF16)
    v_ref[...] = qkv[:, (n_q + n_k) * LANES:].astype(BF16)


def _attn_proj(x, g, w, tabs, seq_len):
    n = x.shape[0]
    tm = TOKEN_TILE
    per_seq = seq_len // tm
    row = pl.BlockSpec((tm, D_MODEL), lambda i: (i, 0))
    tab = pl.BlockSpec((tm, LANES), lambda i: (i % per_seq, 0))
    kv_w = N_KV_HEADS * LANES
    return pl.pallas_call(
        _attn_proj_kernel,
        grid=(n // tm,),
        in_specs=[row, _const_spec((1, D_MODEL)), _const_spec(w.shape), tab, tab, tab],
        out_specs=[row, pl.BlockSpec((tm, kv_w), lambda i: (i, 0)), pl.BlockSpec((tm, kv_w), lambda i: (i, 0))],
        out_shape=[jax.ShapeDtypeStruct((n, D_MODEL), BF16), jax.ShapeDtypeStruct((n, kv_w), BF16),
                   jax.ShapeDtypeStruct((n, kv_w), BF16)],
        compiler_params=_params(("parallel",)),
        name="attn_proj",
    )(x, g, w, *tabs)


def _attn_core_kernel(sink_ref, q_ref, kp_ref, km_ref, kn_ref, vp_ref, vm_ref, vn_ref, x_ref, g_ref, wo_ref,
                      o_ref, kext_ref, vext_ref, obuf_ref, *, n_blocks):
    tq = q_ref.shape[0]
    blocks = tq // ATT_BLOCK
    i = pl.program_id(1)
    blk = ATT_BLOCK
    kext_ref[0:blk] = kp_ref[...]
    kext_ref[blk:blk + tq] = km_ref[...]
    kext_ref[blk + tq:] = kn_ref[...]
    vext_ref[0:blk] = vp_ref[...]
    vext_ref[blk:blk + tq] = vm_ref[...]
    vext_ref[blk + tq:] = vn_ref[...]

    rows = GROUP * blk
    qpos = lax.broadcasted_iota(jnp.int32, (rows, 3 * blk), 0) & (blk - 1)
    col = lax.broadcasted_iota(jnp.int32, (rows, 3 * blk), 1)
    neg_inf = jnp.float32(-jnp.inf)
    band = jnp.where(jnp.abs(col - blk - qpos) <= blk, 0.0, neg_inf)
    col_row = lax.broadcasted_iota(jnp.int32, (1, 3 * blk), 1)
    lane = lax.broadcasted_iota(jnp.int32, (blk, LANES), 1)
    lo = lane < HEAD_DIM

    for qb in range(blocks):
        gb = i * blocks + qb
        edge_lo = jnp.where(gb == 0, blk, 0)
        edge_hi = jnp.where(gb == n_blocks - 1, 2 * blk, 3 * blk)
        edge = jnp.where((col_row >= edge_lo) & (col_row < edge_hi), 0.0, neg_inf)
        bias = band + edge
        r0 = qb * blk
        for g in range(N_KV_HEADS):
            kg = kext_ref[r0:r0 + 3 * blk, g * LANES:(g + 1) * LANES]
            vg = vext_ref[r0:r0 + 3 * blk, g * LANES:(g + 1) * LANES]
            parts = []
            for pair in range(GROUP // 2):
                c0 = (g * GROUP // 2 + pair) * LANES
                qp = q_ref[r0:r0 + blk, c0:c0 + LANES]
                zero = jnp.zeros_like(qp)
                parts.append(jnp.where(lo, qp, zero))
                parts.append(jnp.where(lo, zero, qp))
            qs = jnp.concatenate(parts, axis=0)
            s = lax.dot_general(qs, kg, (((1,), (1,)), ((), ())), preferred_element_type=F32) + bias
            ps, inv = [], []
            for r in range(GROUP):
                sr = s[r * blk:(r + 1) * blk]
                sink = sink_ref[g * GROUP + r]
                m = jnp.maximum(jnp.max(sr, axis=-1, keepdims=True), sink)
                p = jnp.exp(sr - m)
                denom = jnp.sum(p, axis=-1, keepdims=True) + jnp.exp(sink - m)
                ps.append(p.astype(BF16))
                inv.append(1.0 / denom)
            pv = jnp.dot(jnp.concatenate(ps, axis=0), vg, preferred_element_type=F32)
            for pair in range(GROUP // 2):
                a = pv[(2 * pair) * blk:(2 * pair + 1) * blk] * inv[2 * pair]
                b = pv[(2 * pair + 1) * blk:(2 * pair + 2) * blk] * inv[2 * pair + 1]
                c0 = (g * GROUP // 2 + pair) * LANES
                obuf_ref[r0:r0 + blk, c0:c0 + LANES] = jnp.where(lo, a, b).astype(BF16)

    y = jnp.dot(obuf_ref[...], wo_ref[...], preferred_element_type=F32)
    o_ref[...] = x_ref[...] + _rms(y, g_ref[...])


def _attn_core(sink, q, kd, vd, x, g, wo, batch, seq_len):
    tq = ATT_Q_TILE
    blocks = tq // ATT_BLOCK
    n_blocks = seq_len // ATT_BLOCK
    kv_w = N_KV_HEADS * LANES
    q3 = q.reshape(batch, seq_len, D_MODEL)
    k3 = kd.reshape(batch, seq_len, kv_w)
    v3 = vd.reshape(batch, seq_len, kv_w)
    x3 = x.reshape(batch, seq_len, D_MODEL)
    row = pl.BlockSpec((None, tq, D_MODEL), lambda b, i: (b, i, 0))
    kv_main = pl.BlockSpec((None, tq, kv_w), lambda b, i: (b, i, 0))
    kv_prev = pl.BlockSpec((None, ATT_BLOCK, kv_w), lambda b, i: (b, jnp.maximum(i * blocks - 1, 0), 0))
    kv_next = pl.BlockSpec((None, ATT_BLOCK, kv_w), lambda b, i: (b, jnp.minimum((i + 1) * blocks, n_blocks - 1), 0))
    out = pl.pallas_call(
        functools.partial(_attn_core_kernel, n_blocks=n_blocks),
        grid=(batch, seq_len // tq),
        in_specs=[pl.BlockSpec(memory_space=pltpu.SMEM), row, kv_prev, kv_main, kv_next, kv_prev, kv_main, kv_next,
                  row, _const_spec((1, D_MODEL)), _const_spec(wo.shape)],
        out_specs=row,
        out_shape=jax.ShapeDtypeStruct((batch, seq_len, D_MODEL), F32),
        scratch_shapes=[pltpu.VMEM((tq + 2 * ATT_BLOCK, kv_w), BF16), pltpu.VMEM((tq + 2 * ATT_BLOCK, kv_w), BF16),
                        pltpu.VMEM((tq, D_MODEL), BF16)],
        compiler_params=_params(("parallel", "parallel")),
        name="attn_core",
    )(sink, q3, k3, k3, k3, v3, v3, v3, x3, g, wo)
    return out.reshape(batch * seq_len, D_MODEL)


def _ret_proj_kernel(x_ref, g_ref, w_ref, cos_ref, sin_ref, o_ref):
    h = _rms(x_ref[...], g_ref[...]).astype(BF16)
    cos, sin = cos_ref[...], sin_ref[...]
    half = RET_QK_DIM // 2
    n_chunks = RET_IN_DIM // D_MODEL
    for c in range(n_chunks):
        y = jnp.dot(h, w_ref[:, c * D_MODEL:(c + 1) * D_MODEL], preferred_element_type=F32)
        if c < 2:
            scale = 1.0 if c == 0 else RET_QK_DIM ** -0.5
            for hd in range(RET_HEADS):
                x1 = y[:, hd * RET_QK_DIM:hd * RET_QK_DIM + half]
                x2 = y[:, hd * RET_QK_DIM + half:(hd + 1) * RET_QK_DIM]
                base = c * D_MODEL + hd * RET_QK_DIM
                o_ref[:, base:base + half] = ((x1 * cos - x2 * sin) * scale).astype(BF16)
                o_ref[:, base + half:base + 2 * half] = ((x2 * cos + x1 * sin) * scale).astype(BF16)
        else:
            o_ref[:, c * D_MODEL:(c + 1) * D_MODEL] = y.astype(BF16)


def _ret_proj(x, g, w, tabs, seq_len):
    n = x.shape[0]
    tm = TOKEN_TILE
    per_seq = seq_len // tm
    row = pl.BlockSpec((tm, D_MODEL), lambda i: (i, 0))
    tab = pl.BlockSpec((tm, LANES), lambda i: (i % per_seq, 0))
    return pl.pallas_call(
        _ret_proj_kernel,
        grid=(n // tm,),
        in_specs=[row, _const_spec((1, D_MODEL)), _const_spec(w.shape), tab, tab],
        out_specs=pl.BlockSpec((tm, RET_IN_DIM), lambda i: (i, 0)),
        out_shape=jax.ShapeDtypeStruct((n, RET_IN_DIM), BF16),
        compiler_params=_params(("parallel",)),
        name="ret_proj",
    )(x, g, w, *tabs)


def _ret_core_kernel(lgf_ref, lgb_ref, q_ref, k_ref, v_ref, g_ref, o_ref, acc_ref, st_ref):
    seq_len = q_ref.shape[0]
    n_chunks = seq_len // CHUNK
    hd = pl.program_id(1)
    lgf = lgf_ref[hd]
    lgb = lgb_ref[hd]
    ii = lax.broadcasted_iota(jnp.int32, (CHUNK, CHUNK), 0)
    jj = lax.broadcasted_iota(jnp.int32, (CHUNK, CHUNK), 1)
    diff = (ii - jj).astype(F32)
    dmat = jnp.where(diff >= 0, jnp.exp(lgf * jnp.maximum(diff, 0.0)), jnp.exp(lgb * jnp.maximum(-diff, 0.0)))
    idx = lax.broadcasted_iota(jnp.int32, (CHUNK, 1), 0).astype(F32)
    qdec_f = jnp.exp(lgf * (idx + 1.0))
    kdec_f = jnp.exp(lgf * (CHUNK - 1.0 - idx))
    cdec_f = jnp.exp(lgf * CHUNK)
    qdec_b = jnp.exp(lgb * (CHUNK - idx))
    kdec_b = jnp.exp(lgb * idx)
    cdec_b = jnp.exp(lgb * CHUNK)

    def state_update(kc, vc, kdec, cdec):
        kd = (kc.astype(F32) * kdec).astype(BF16)
        upd = lax.dot_general(kd, vc, (((0,), (0,)), ((), ())), preferred_element_type=F32)
        st_ref[...] = cdec * st_ref[...] + upd

    st_ref[...] = jnp.zeros_like(st_ref)

    def fwd(c, carry):
        rows = pl.ds(pl.multiple_of(c * CHUNK, CHUNK), CHUNK)
        qc, kc, vc = q_ref[rows, :], k_ref[rows, :], v_ref[rows, :]
        s = lax.dot_general(qc, kc, (((1,), (1,)), ((), ())), preferred_element_type=F32) * dmat
        intra = jnp.dot(s.astype(BF16), vc, preferred_element_type=F32)
        cross = jnp.dot(qc, st_ref[...].astype(BF16), preferred_element_type=F32) * qdec_f
        acc_ref[rows, :] = intra + cross
        state_update(kc, vc, kdec_f, cdec_f)
        return carry

    lax.fori_loop(0, n_chunks, fwd, 0, unroll=RET_UNROLL)

    st_ref[...] = jnp.zeros_like(st_ref)

    def bwd(t, carry):
        c = n_chunks - 1 - t
        rows = pl.ds(pl.multiple_of(c * CHUNK, CHUNK), CHUNK)
        qc, kc, vc = q_ref[rows, :], k_ref[rows, :], v_ref[rows, :]
        cross = jnp.dot(qc, st_ref[...].astype(BF16), preferred_element_type=F32) * qdec_b
        y = acc_ref[rows, :] + cross
        yn = y * lax.rsqrt(jnp.mean(y * y, axis=-1, keepdims=True) + NORM_EPS)
        gate = g_ref[rows, :].astype(F32)
        o_ref[rows, :] = (gate * jax.nn.sigmoid(gate) * yn).astype(BF16)
        state_update(kc, vc, kdec_b, cdec_b)
        return carry

    lax.fori_loop(0, n_chunks, bwd, 0, unroll=RET_UNROLL)


def _ret_core(lgf, lgb, proj, batch, seq_len):
    p3 = proj.reshape(batch, seq_len, RET_IN_DIM)
    qk_blocks = D_MODEL // RET_QK_DIM
    v_off = 2 * D_MODEL // RET_V_DIM
    g_off = v_off + RET_V_TOTAL // RET_V_DIM
    smem = pl.BlockSpec(memory_space=pltpu.SMEM)
    out = pl.pallas_call(
        _ret_core_kernel,
        grid=(batch, RET_HEADS),
        in_specs=[smem, smem,
                  pl.BlockSpec((None, seq_len, RET_QK_DIM), lambda b, h: (b, 0, h)),
                  pl.BlockSpec((None, seq_len, RET_QK_DIM), lambda b, h: (b, 0, qk_blocks + h)),
                  pl.BlockSpec((None, seq_len, RET_V_DIM), lambda b, h: (b, 0, v_off + h)),
                  pl.BlockSpec((None, seq_len, RET_V_DIM), lambda b, h: (b, 0, g_off + h))],
        out_specs=pl.BlockSpec((None, seq_len, RET_V_DIM), lambda b, h: (b, 0, h)),
        out_shape=jax.ShapeDtypeStruct((batch, seq_len, RET_V_TOTAL), BF16),
        scratch_shapes=[pltpu.VMEM((seq_len, RET_V_DIM), F32), pltpu.VMEM((RET_QK_DIM, RET_V_DIM), F32)],
        compiler_params=_params(("parallel", "parallel")),
        name="ret_core",
    )(lgf, lgb, p3, p3, p3, p3)
    return out.reshape(batch * seq_len, RET_V_TOTAL)


def _ret_out_kernel(y_ref, x_ref, g_ref, w_ref, o_ref):
    y = jnp.dot(y_ref[...], w_ref[...], preferred_element_type=F32)
    o_ref[...] = x_ref[...] + _rms(y, g_ref[...])


def _ret_out(y, x, g, w):
    n = x.shape[0]
    tm = TOKEN_TILE
    row = pl.BlockSpec((tm, D_MODEL), lambda i: (i, 0))
    return pl.pallas_call(
        _ret_out_kernel,
        grid=(n // tm,),
        in_specs=[pl.BlockSpec((tm, RET_V_TOTAL), lambda i: (i, 0)), row, _const_spec((1, D_MODEL)),
                  _const_spec(w.shape)],
        out_specs=row,
        out_shape=jax.ShapeDtypeStruct((n, D_MODEL), F32),
        compiler_params=_params(("parallel",)),
        name="ret_out",
    )(y, x, g, w)


def _attn_tables(seq_len):
    half = ROT_DIM // 2
    inv_freq = ROPE_THETA ** (-jnp.arange(half, dtype=F32) / half)
    ang = jnp.arange(seq_len, dtype=F32)[:, None] * inv_freq[None, :]
    cos, sin = jnp.cos(ang), jnp.sin(ang)
    pad = HEAD_DIM - ROT_DIM
    ones = jnp.ones((seq_len, pad), F32)
    zeros = jnp.zeros((seq_len, pad), F32)
    zh = jnp.zeros((seq_len, half), F32)
    c = jnp.concatenate([cos, cos, ones], axis=1)
    s1 = jnp.concatenate([-sin, zh, zeros], axis=1)
    s2 = jnp.concatenate([zh, sin, zeros], axis=1)
    reps = LANES // HEAD_DIM
    return tuple(jnp.tile(t, (1, reps)) for t in (c, s1, s2))


def _ret_tables(seq_len):
    half = RET_QK_DIM // 2
    inv_freq = RET_THETA ** (-jnp.arange(half, dtype=F32) / half)
    ang = jnp.arange(seq_len, dtype=F32)[:, None] * inv_freq[None, :]
    return jnp.cos(ang), jnp.sin(ang)


def _attn_weights(w_qkv):
    q_dim = N_HEADS * HEAD_DIM
    kv_dim = N_KV_HEADS * HEAD_DIM
    wq = w_qkv[:, :q_dim]
    wk = w_qkv[:, q_dim:q_dim + kv_dim].reshape(D_MODEL, N_KV_HEADS, 1, HEAD_DIM)
    wv = w_qkv[:, q_dim + kv_dim:].reshape(D_MODEL, N_KV_HEADS, 1, HEAD_DIM)
    reps = LANES // HEAD_DIM
    wk = jnp.broadcast_to(wk, (D_MODEL, N_KV_HEADS, reps, HEAD_DIM)).reshape(D_MODEL, N_KV_HEADS * LANES)
    wv = jnp.broadcast_to(wv, (D_MODEL, N_KV_HEADS, reps, HEAD_DIM)).reshape(D_MODEL, N_KV_HEADS * LANES)
    return jnp.concatenate([wq, wk, wv], axis=1).astype(BF16)


def _trunk(x, gains, ffn_w, attn_w, attn_wo, sink, ret_w, ret_wo, lgf, lgb, attn_tabs, ret_tabs):
    batch, seq_len, _ = x.shape
    x = x.reshape(batch * seq_len, D_MODEL)
    for layer in range(2):
        g = gains[layer]
        x = _ffn(x, g[0], g[1], *ffn_w[layer][0])
        if layer == 0:
            q, kd, vd = _attn_proj(x, g[2], attn_w, attn_tabs, seq_len)
            x = _attn_core(sink, q, kd, vd, x, g[3], attn_wo, batch, seq_len)
        else:
            proj = _ret_proj(x, g[2], ret_w, ret_tabs, seq_len)
            y = _ret_core(lgf, lgb, proj, batch, seq_len)
            x = _ret_out(y, x, g[3], ret_wo)
        x = _ffn(x, g[4], g[5], *ffn_w[layer][1])
    return x.reshape(batch, seq_len, D_MODEL)


def kernel(x_prompt, x_sample, norm_gains, ffn_w_in, ffn_w_out, attn_w_qkv, attn_w_o, attn_sink, ret_w_in, ret_w_o, ret_decay_fwd, ret_decay_bwd):
    gains = norm_gains.astype(F32).reshape(2, 6, 1, D_MODEL)
    ffn_w = [[(ffn_w_in[l, s].astype(BF16), ffn_w_out[l, s].astype(BF16)) for s in range(2)] for l in range(2)]
    attn_w = _attn_weights(attn_w_qkv[0])
    attn_wo = attn_w_o[0].astype(BF16)
    sink = attn_sink[0].astype(F32)
    ret_w = ret_w_in[0].astype(BF16)
    ret_wo = ret_w_o[0].astype(BF16)
    lgf = jax.nn.log_sigmoid(ret_decay_fwd[0].astype(F32))
    lgb = jax.nn.log_sigmoid(ret_decay_bwd[0].astype(F32))
    outs = []
    for x in (x_prompt, x_sample):
        seq_len = x.shape[1]
        outs.append(_trunk(x, gains, ffn_w, attn_w, attn_wo, sink, ret_w, ret_wo, lgf, lgb,
                           _attn_tables(seq_len), _ret_tables(seq_len)))
    return tuple(outs)
```

```python
import functools

import jax
import jax.numpy as jnp
import numpy as np
from jax import lax
from jax.experimental import pallas as pl
from jax.experimental.pallas import tpu as pltpu

D_MODEL = 1024
HEAD_DIM = 64
N_HEADS = 16
N_KV_HEADS = 4
GROUP = 4
ATT_BLOCK = 128
ROT_DIM = 16
ROPE_THETA = 500000.0
RET_HEADS = 4
RET_QK_DIM = 256
RET_V_DIM = 512
RET_V_TOTAL = 2048
RET_IN_DIM = 6144
RET_THETA = 10000.0
D_FF = 2816
NORM_EPS = 1e-6

LANES = 128
FF_TILE = 256
N_FF_TILES = D_FF // FF_TILE
TOKEN_TILE = 512
PROJ_SUB_BLOCKS = 2
FFN_TOKEN_TILE = 1024
FFN_SUB_BLOCKS = 4
ATT_Q_TILE = 512
RET_BLOCK = 256
RET_UNROLL = 8
VMEM_LIMIT = 56 * 1024 * 1024

F32 = jnp.float32
BF16 = jnp.bfloat16


def _rms(x, g):
    return x * lax.rsqrt(jnp.mean(x * x, axis=-1, keepdims=True) + NORM_EPS) * g


def _const_spec(shape):
    zeros = (0,) * len(shape)
    return pl.BlockSpec(shape, lambda *_: zeros, pipeline_mode=pl.Buffered(1))


def _params(sem):
    return pltpu.CompilerParams(dimension_semantics=sem, vmem_limit_bytes=VMEM_LIMIT)


def _ffn_kernel(x_ref, gpre_ref, gpost_ref, wi_ref, wo_ref, o_ref, act_ref):
    sub = x_ref.shape[0] // FFN_SUB_BLOCKS
    for s in range(FFN_SUB_BLOCKS):
        rows = slice(s * sub, (s + 1) * sub)
        h = _rms(x_ref[rows, :], gpre_ref[...]).astype(BF16)
        for f in range(N_FF_TILES):
            gate = jnp.dot(h, wi_ref[:, f * FF_TILE:(f + 1) * FF_TILE], preferred_element_type=F32)
            up = jnp.dot(h, wi_ref[:, D_FF + f * FF_TILE:D_FF + (f + 1) * FF_TILE], preferred_element_type=F32)
            act_ref[rows, f * FF_TILE:(f + 1) * FF_TILE] = (gate * jax.nn.sigmoid(gate) * up).astype(BF16)
    for s in range(FFN_SUB_BLOCKS):
        rows = slice(s * sub, (s + 1) * sub)
        y = jnp.dot(act_ref[rows, :], wo_ref[...], preferred_element_type=F32)
        o_ref[rows, :] = x_ref[rows, :] + 0.5 * _rms(y, gpost_ref[...])


def _ffn(x, g_pre, g_post, wi, wo):
    n = x.shape[0]
    tm = FFN_TOKEN_TILE
    row = pl.BlockSpec((tm, D_MODEL), lambda i: (i, 0))
    return pl.pallas_call(
        _ffn_kernel,
        grid=(n // tm,),
        in_specs=[row, _const_spec((1, D_MODEL)), _const_spec((1, D_MODEL)),
                  _const_spec(wi.shape), _const_spec(wo.shape)],
        out_specs=row,
        out_shape=jax.ShapeDtypeStruct((n, D_MODEL), F32),
        scratch_shapes=[pltpu.VMEM((tm, D_FF), BF16)],
        compiler_params=_params(("parallel",)),
        name="ffn",
    )(x, g_pre, g_post, wi, wo)


def _attn_proj_kernel(x_ref, g_ref, w_ref, c_ref, s1_ref, s2_ref, q_ref, k_ref, v_ref):
    h = _rms(x_ref[...], g_ref[...]).astype(BF16)
    qkv = jnp.dot(h, w_ref[...], preferred_element_type=F32)
    c, s1, s2 = c_ref[...], s1_ref[...], s2_ref[...]
    n_q = N_HEADS * HEAD_DIM // LANES
    n_k = N_KV_HEADS
    for j in range(n_q + n_k):
        xj = qkv[:, j * LANES:(j + 1) * LANES]
        rot = xj * c + pltpu.roll(xj, LANES - ROT_DIM // 2, 1) * s1 + pltpu.roll(xj, ROT_DIM // 2, 1) * s2
        if j < n_q:
            q_ref[:, j * LANES:(j + 1) * LANES] = (rot * (HEAD_DIM ** -0.5)).astype(BF16)
        else:
            jk = j - n_q
            k_ref[:, jk * LANES:(jk + 1) * LANES] = rot.astype(BF16)
    v_ref[...] = qkv[:, (n_q + n_k) * LANES:].astype(BF16)


def _attn_proj(x, g, w, tabs, seq_len):
    n = x.shape[0]
    tm = TOKEN_TILE
    per_seq = seq_len // tm
    row = pl.BlockSpec((tm, D_MODEL), lambda i: (i, 0))
    tab = pl.BlockSpec((tm, LANES), lambda i: (i % per_seq, 0))
    kv_w = N_KV_HEADS * LANES
    return pl.pallas_call(
        _attn_proj_kernel,
        grid=(n // tm,),
        in_specs=[row, _const_spec((1, D_MODEL)), _const_spec(w.shape), tab, tab, tab],
        out_specs=[row, pl.BlockSpec((tm, kv_w), lambda i: (i, 0)), pl.BlockSpec((tm, kv_w), lambda i: (i, 0))],
        out_shape=[jax.ShapeDtypeStruct((n, D_MODEL), BF16), jax.ShapeDtypeStruct((n, kv_w), BF16),
                   jax.ShapeDtypeStruct((n, kv_w), BF16)],
        compiler_params=_params(("parallel",)),
        name="attn_proj",
    )(x, g, w, *tabs)


def _attn_core_kernel(sink_ref, q_ref, kp_ref, km_ref, kn_ref, vp_ref, vm_ref, vn_ref, x_ref, g_ref, wo_ref,
                      o_ref, kext_ref, vext_ref, obuf_ref, *, n_blocks):
    tq = q_ref.shape[0]
    blocks = tq // ATT_BLOCK
    i = pl.program_id(1)
    blk = ATT_BLOCK
    kext_ref[0:blk] = kp_ref[...]
    kext_ref[blk:blk + tq] = km_ref[...]
    kext_ref[blk + tq:] = kn_ref[...]
    vext_ref[0:blk] = vp_ref[...]
    vext_ref[blk:blk + tq] = vm_ref[...]
    vext_ref[blk + tq:] = vn_ref[...]

    rows = GROUP * blk
    qpos = lax.broadcasted_iota(jnp.int32, (rows, 3 * blk), 0) & (blk - 1)
    col = lax.broadcasted_iota(jnp.int32, (rows, 3 * blk), 1)
    neg_inf = jnp.float32(-jnp.inf)
    band = jnp.where(jnp.abs(col - blk - qpos) <= blk, 0.0, neg_inf)
    col_row = lax.broadcasted_iota(jnp.int32, (1, 3 * blk), 1)
    lane = lax.broadcasted_iota(jnp.int32, (blk, LANES), 1)
    lo = lane < HEAD_DIM

    for qb in range(blocks):
        gb = i * blocks + qb
        edge_lo = jnp.where(gb == 0, blk, 0)
        edge_hi = jnp.where(gb == n_blocks - 1, 2 * blk, 3 * blk)
        edge = jnp.where((col_row >= edge_lo) & (col_row < edge_hi), 0.0, neg_inf)
        bias = band + edge
        r0 = qb * blk
        for g in range(N_KV_HEADS):
            kg = kext_ref[r0:r0 + 3 * blk, g * LANES:(g + 1) * LANES]
            vg = vext_ref[r0:r0 + 3 * blk, g * LANES:(g + 1) * LANES]
            parts = []
            for pair in range(GROUP // 2):
                c0 = (g * GROUP // 2 + pair) * LANES
                qp = q_ref[r0:r0 + blk, c0:c0 + LANES]
                zero = jnp.zeros_like(qp)
                parts.append(jnp.where(lo, qp, zero))
                parts.append(jnp.where(lo, zero, qp))
            qs = jnp.concatenate(parts, axis=0)
            s = lax.dot_general(qs, kg, (((1,), (1,)), ((), ())), preferred_element_type=F32) + bias
            ps, inv = [], []
            for r in range(GROUP):
                sr = s[r * blk:(r + 1) * blk]
                sink = sink_ref[g * GROUP + r]
                m = jnp.maximum(jnp.max(sr, axis=-1, keepdims=True), sink)
                p = jnp.exp(sr - m)
                denom = jnp.sum(p, axis=-1, keepdims=True) + jnp.exp(sink - m)
                ps.append(p.astype(BF16))
                inv.append(1.0 / denom)
            pv = jnp.dot(jnp.concatenate(ps, axis=0), vg, preferred_element_type=F32)
            for pair in range(GROUP // 2):
                a = pv[(2 * pair) * blk:(2 * pair + 1) * blk] * inv[2 * pair]
                b = pv[(2 * pair + 1) * blk:(2 * pair + 2) * blk] * inv[2 * pair + 1]
                c0 = (g * GROUP // 2 + pair) * LANES
                obuf_ref[r0:r0 + blk, c0:c0 + LANES] = jnp.where(lo, a, b).astype(BF16)

    y = jnp.dot(obuf_ref[...], wo_ref[...], preferred_element_type=F32)
    o_ref[...] = x_ref[...] + _rms(y, g_ref[...])


def _attn_core(sink, q, kd, vd, x, g, wo, batch, seq_len):
    tq = ATT_Q_TILE
    blocks = tq // ATT_BLOCK
    n_blocks = seq_len // ATT_BLOCK
    kv_w = N_KV_HEADS * LANES
    q3 = q.reshape(batch, seq_len, D_MODEL)
    k3 = kd.reshape(batch, seq_len, kv_w)
    v3 = vd.reshape(batch, seq_len, kv_w)
    x3 = x.reshape(batch, seq_len, D_MODEL)
    row = pl.BlockSpec((None, tq, D_MODEL), lambda b, i: (b, i, 0))
    kv_main = pl.BlockSpec((None, tq, kv_w), lambda b, i: (b, i, 0))
    kv_prev = pl.BlockSpec((None, ATT_BLOCK, kv_w), lambda b, i: (b, jnp.maximum(i * blocks - 1, 0), 0))
    kv_next = pl.BlockSpec((None, ATT_BLOCK, kv_w), lambda b, i: (b, jnp.minimum((i + 1) * blocks, n_blocks - 1), 0))
    out = pl.pallas_call(
        functools.partial(_attn_core_kernel, n_blocks=n_blocks),
        grid=(batch, seq_len // tq),
        in_specs=[pl.BlockSpec(memory_space=pltpu.SMEM), row, kv_prev, kv_main, kv_next, kv_prev, kv_main, kv_next,
                  row, _const_spec((1, D_MODEL)), _const_spec(wo.shape)],
        out_specs=row,
        out_shape=jax.ShapeDtypeStruct((batch, seq_len, D_MODEL), F32),
        scratch_shapes=[pltpu.VMEM((tq + 2 * ATT_BLOCK, kv_w), BF16), pltpu.VMEM((tq + 2 * ATT_BLOCK, kv_w), BF16),
                        pltpu.VMEM((tq, D_MODEL), BF16)],
        compiler_params=_params(("parallel", "parallel")),
        name="attn_core",
    )(sink, q3, k3, k3, k3, v3, v3, v3, x3, g, wo)
    return out.reshape(batch * seq_len, D_MODEL)


def _ret_proj_kernel(x_ref, g_ref, w_ref, cos_ref, sin_ref, o_ref):
    half = RET_QK_DIM // 2
    n_chunks = RET_IN_DIM // D_MODEL
    sub = x_ref.shape[0] // PROJ_SUB_BLOCKS
    for s in range(PROJ_SUB_BLOCKS):
        rows = slice(s * sub, (s + 1) * sub)
        h = _rms(x_ref[rows, :], g_ref[...]).astype(BF16)
        cos, sin = cos_ref[rows, :], sin_ref[rows, :]
        for c in range(n_chunks):
            y = jnp.dot(h, w_ref[:, c * D_MODEL:(c + 1) * D_MODEL], preferred_element_type=F32)
            if c < 2:
                scale = 1.0 if c == 0 else RET_QK_DIM ** -0.5
                for hd in range(RET_HEADS):
                    x1 = y[:, hd * RET_QK_DIM:hd * RET_QK_DIM + half]
                    x2 = y[:, hd * RET_QK_DIM + half:(hd + 1) * RET_QK_DIM]
                    base = c * D_MODEL + hd * RET_QK_DIM
                    o_ref[rows, base:base + half] = ((x1 * cos - x2 * sin) * scale).astype(BF16)
                    o_ref[rows, base + half:base + 2 * half] = ((x2 * cos + x1 * sin) * scale).astype(BF16)
            elif c < 4:
                o_ref[rows, c * D_MODEL:(c + 1) * D_MODEL] = y.astype(BF16)
            else:
                o_ref[rows, c * D_MODEL:(c + 1) * D_MODEL] = (y * jax.nn.sigmoid(y)).astype(BF16)


def _ret_proj(x, g, w, tabs, seq_len):
    n = x.shape[0]
    tm = TOKEN_TILE
    per_seq = seq_len // tm
    row = pl.BlockSpec((tm, D_MODEL), lambda i: (i, 0))
    tab = pl.BlockSpec((tm, LANES), lambda i: (i % per_seq, 0))
    return pl.pallas_call(
        _ret_proj_kernel,
        grid=(n // tm,),
        in_specs=[row, _const_spec((1, D_MODEL)), _const_spec(w.shape), tab, tab],
        out_specs=pl.BlockSpec((tm, RET_IN_DIM), lambda i: (i, 0)),
        out_shape=jax.ShapeDtypeStruct((n, RET_IN_DIM), BF16),
        compiler_params=_params(("parallel",)),
        name="ret_proj",
    )(x, g, w, *tabs)


def _ret_core_kernel(lgf_ref, lgb_ref, q_ref, k_ref, v_ref, o_ref, acc_ref, st_ref, dmat_ref):
    blk = RET_BLOCK
    seq_len = q_ref.shape[0]
    n_chunks = seq_len // blk
    hd = pl.program_id(1)
    lgf = lgf_ref[hd]
    lgb = lgb_ref[hd]
    ii = lax.broadcasted_iota(jnp.int32, (blk, blk), 0)
    jj = lax.broadcasted_iota(jnp.int32, (blk, blk), 1)
    diff = (ii - jj).astype(F32)
    dmat_ref[...] = jnp.where(diff >= 0, jnp.exp(lgf * jnp.maximum(diff, 0.0)),
                              jnp.exp(lgb * jnp.maximum(-diff, 0.0)))
    idx = lax.broadcasted_iota(jnp.int32, (blk, 1), 0).astype(F32)
    qdec_f = jnp.exp(lgf * (idx + 1.0))
    kdec_f = jnp.exp(lgf * (blk - 1.0 - idx))
    cdec_f = jnp.exp(lgf * blk)
    qdec_b = jnp.exp(lgb * (blk - idx))
    kdec_b = jnp.exp(lgb * idx)
    cdec_b = jnp.exp(lgb * blk)

    def state_update(kc, vc, kdec, cdec):
        kd = (kc.astype(F32) * kdec).astype(BF16)
        upd = lax.dot_general(kd, vc, (((0,), (0,)), ((), ())), preferred_element_type=F32)
        st_ref[...] = cdec * st_ref[...] + upd

    st_ref[...] = jnp.zeros_like(st_ref)

    def fwd(c, carry):
        rows = pl.ds(pl.multiple_of(c * blk, blk), blk)
        qc, kc, vc = q_ref[rows, :], k_ref[rows, :], v_ref[rows, :]
        s = lax.dot_general(qc, kc, (((1,), (1,)), ((), ())), preferred_element_type=F32) * dmat_ref[...]
        intra = jnp.dot(s.astype(BF16), vc, preferred_element_type=F32)
        cross = jnp.dot(qc, st_ref[...].astype(BF16), preferred_element_type=F32) * qdec_f
        acc_ref[rows, :] = intra + cross
        state_update(kc, vc, kdec_f, cdec_f)
        return carry

    lax.fori_loop(0, n_chunks, fwd, 0, unroll=RET_UNROLL)

    st_ref[...] = jnp.zeros_like(st_ref)

    def bwd(t, carry):
        c = n_chunks - 1 - t
        rows = pl.ds(pl.multiple_of(c * blk, blk), blk)
        qc, kc, vc = q_ref[rows, :], k_ref[rows, :], v_ref[rows, :]
        cross = jnp.dot(qc, st_ref[...].astype(BF16), preferred_element_type=F32) * qdec_b
        o_ref[rows, :] = (acc_ref[rows, :] + cross).astype(BF16)
        state_update(kc, vc, kdec_b, cdec_b)
        return carry

    lax.fori_loop(0, n_chunks, bwd, 0, unroll=RET_UNROLL)


def _ret_core(lgf, lgb, proj, batch, seq_len):
    p3 = proj.reshape(batch, seq_len, RET_IN_DIM)
    qk_blocks = D_MODEL // RET_QK_DIM
    v_off = 2 * D_MODEL // RET_V_DIM
    smem = pl.BlockSpec(memory_space=pltpu.SMEM)
    out = pl.pallas_call(
        _ret_core_kernel,
        grid=(batch, RET_HEADS),
        in_specs=[smem, smem,
                  pl.BlockSpec((None, seq_len, RET_QK_DIM), lambda b, h: (b, 0, h)),
                  pl.BlockSpec((None, seq_len, RET_QK_DIM), lambda b, h: (b, 0, qk_blocks + h)),
                  pl.BlockSpec((None, seq_len, RET_V_DIM), lambda b, h: (b, 0, v_off + h))],
        out_specs=pl.BlockSpec((None, seq_len, RET_V_DIM), lambda b, h: (b, 0, h)),
        out_shape=jax.ShapeDtypeStruct((batch, seq_len, RET_V_TOTAL), BF16),
        scratch_shapes=[pltpu.VMEM((seq_len, RET_V_DIM), F32), pltpu.VMEM((RET_QK_DIM, RET_V_DIM), F32),
                        pltpu.VMEM((RET_BLOCK, RET_BLOCK), F32)],
        compiler_params=_params(("parallel", "parallel")),
        name="ret_core",
    )(lgf, lgb, p3, p3, p3)
    return out.reshape(batch * seq_len, RET_V_TOTAL)


def _ret_out_kernel(y_ref, gate_ref, x_ref, g_ref, w_ref, o_ref, z_ref):
    sub = x_ref.shape[0] // PROJ_SUB_BLOCKS
    for s in range(PROJ_SUB_BLOCKS):
        rows = slice(s * sub, (s + 1) * sub)
        for hd in range(RET_HEADS):
            cols = slice(hd * RET_V_DIM, (hd + 1) * RET_V_DIM)
            y = y_ref[rows, cols].astype(F32)
            yn = y * lax.rsqrt(jnp.mean(y * y, axis=-1, keepdims=True) + NORM_EPS)
            z_ref[rows, cols] = (gate_ref[rows, cols].astype(F32) * yn).astype(BF16)
        out = jnp.dot(z_ref[rows, :], w_ref[...], preferred_element_type=F32)
        o_ref[rows, :] = x_ref[rows, :] + _rms(out, g_ref[...])


def _ret_out(y, proj, x, g, w):
    n = x.shape[0]
    tm = TOKEN_TILE
    row = pl.BlockSpec((tm, D_MODEL), lambda i: (i, 0))
    wide = pl.BlockSpec((tm, RET_V_TOTAL), lambda i: (i, 0))
    gate_cols = pl.BlockSpec((tm, RET_V_TOTAL), lambda i: (i, RET_IN_DIM // RET_V_TOTAL - 1))
    return pl.pallas_call(
        _ret_out_kernel,
        grid=(n // tm,),
        in_specs=[wide, gate_cols, row, _const_spec((1, D_MODEL)), _const_spec(w.shape)],
        out_specs=row,
        out_shape=jax.ShapeDtypeStruct((n, D_MODEL), F32),
        scratch_shapes=[pltpu.VMEM((tm, RET_V_TOTAL), BF16)],
        compiler_params=_params(("parallel",)),
        name="ret_out",
    )(y, proj, x, g, w)


def _attn_tables(seq_len):
    half = ROT_DIM // 2
    inv_freq = ROPE_THETA ** (-jnp.arange(half, dtype=F32) / half)
    ang = jnp.arange(seq_len, dtype=F32)[:, None] * inv_freq[None, :]
    cos, sin = jnp.cos(ang), jnp.sin(ang)
    pad = HEAD_DIM - ROT_DIM
    ones = jnp.ones((seq_len, pad), F32)
    zeros = jnp.zeros((seq_len, pad), F32)
    zh = jnp.zeros((seq_len, half), F32)
    c = jnp.concatenate([cos, cos, ones], axis=1)
    s1 = jnp.concatenate([-sin, zh, zeros], axis=1)
    s2 = jnp.concatenate([zh, sin, zeros], axis=1)
    reps = LANES // HEAD_DIM
    return tuple(jnp.tile(t, (1, reps)) for t in (c, s1, s2))


def _ret_tables(seq_len):
    half = RET_QK_DIM // 2
    inv_freq = RET_THETA ** (-jnp.arange(half, dtype=F32) / half)
    ang = jnp.arange(seq_len, dtype=F32)[:, None] * inv_freq[None, :]
    return jnp.cos(ang), jnp.sin(ang)


def _attn_weights(w_qkv):
    q_dim = N_HEADS * HEAD_DIM
    kv_dim = N_KV_HEADS * HEAD_DIM
    wq = w_qkv[:, :q_dim]
    wk = w_qkv[:, q_dim:q_dim + kv_dim].reshape(D_MODEL, N_KV_HEADS, 1, HEAD_DIM)
    wv = w_qkv[:, q_dim + kv_dim:].reshape(D_MODEL, N_KV_HEADS, 1, HEAD_DIM)
    reps = LANES // HEAD_DIM
    wk = jnp.broadcast_to(wk, (D_MODEL, N_KV_HEADS, reps, HEAD_DIM)).reshape(D_MODEL, N_KV_HEADS * LANES)
    wv = jnp.broadcast_to(wv, (D_MODEL, N_KV_HEADS, reps, HEAD_DIM)).reshape(D_MODEL, N_KV_HEADS * LANES)
    return jnp.concatenate([wq, wk, wv], axis=1).astype(BF16)


def _trunk(x, gains, ffn_w, attn_w, attn_wo, sink, ret_w, ret_wo, lgf, lgb, attn_tabs, ret_tabs):
    batch, seq_len, _ = x.shape
    x = x.reshape(batch * seq_len, D_MODEL)
    for layer in range(2):
        g = gains[layer]
        x = _ffn(x, g[0], g[1], *ffn_w[layer][0])
        if layer == 0:
            q, kd, vd = _attn_proj(x, g[2], attn_w, attn_tabs, seq_len)
            x = _attn_core(sink, q, kd, vd, x, g[3], attn_wo, batch, seq_len)
        else:
            proj = _ret_proj(x, g[2], ret_w, ret_tabs, seq_len)
            y = _ret_core(lgf, lgb, proj, batch, seq_len)
            x = _ret_out(y, proj, x, g[3], ret_wo)
        x = _ffn(x, g[4], g[5], *ffn_w[layer][1])
    return x.reshape(batch, seq_len, D_MODEL)


def kernel(x_prompt, x_sample, norm_gains, ffn_w_in, ffn_w_out, attn_w_qkv, attn_w_o, attn_sink, ret_w_in, ret_w_o, ret_decay_fwd, ret_decay_bwd):
    gains = norm_gains.astype(F32).reshape(2, 6, 1, D_MODEL)
    ffn_w = [[(ffn_w_in[l, s].astype(BF16), ffn_w_out[l, s].astype(BF16)) for s in range(2)] for l in range(2)]
    attn_w = _attn_weights(attn_w_qkv[0])
    attn_wo = attn_w_o[0].astype(BF16)
    sink = attn_sink[0].astype(F32)
    ret_w = ret_w_in[0].astype(BF16)
    ret_wo = ret_w_o[0].astype(BF16)
    lgf = jax.nn.log_sigmoid(ret_decay_fwd[0].astype(F32))
    lgb = jax.nn.log_sigmoid(ret_decay_bwd[0].astype(F32))
    outs = []
    for x in (x_prompt, x_sample):
        seq_len = x.shape[1]
        outs.append(_trunk(x, gains, ffn_w, attn_w, attn_wo, sink, ret_w, ret_wo, lgf, lgb,
                           _attn_tables(seq_len), _ret_tables(seq_len)))
    return tuple(outs)
```

```python
import functools

import jax
import jax.numpy as jnp
import numpy as np
from jax import lax
from jax.experimental import pallas as pl
from jax.experimental.pallas import tpu as pltpu

D_MODEL = 1024
HEAD_DIM = 64
N_HEADS = 16
N_KV_HEADS = 4
GROUP = 4
ATT_BLOCK = 128
ROT_DIM = 16
ROPE_THETA = 500000.0
RET_HEADS = 4
RET_QK_DIM = 256
RET_V_DIM = 512
RET_V_TOTAL = 2048
RET_IN_DIM = 6144
RET_THETA = 10000.0
D_FF = 2816
NORM_EPS = 1e-6

LANES = 128
FF_TILE = 256
N_FF_TILES = D_FF // FF_TILE
TOKEN_TILE = 512
PROJ_SUB_BLOCKS = 2
FFN_TOKEN_TILE = 1024
FFN_SUB_BLOCKS = 4
ATT_Q_TILE = 512
RET_BLOCK = 256
RET_UNROLL = 8
VMEM_LIMIT = 56 * 1024 * 1024

LOG2_E = 1.4426950408889634
ATT_Q_SCALE = HEAD_DIM ** -0.5 * LOG2_E

F32 = jnp.float32
BF16 = jnp.bfloat16


def _rms(x, g):
    return x * lax.rsqrt(jnp.mean(x * x, axis=-1, keepdims=True) + NORM_EPS) * g


def _const_spec(shape):
    zeros = (0,) * len(shape)
    return pl.BlockSpec(shape, lambda *_: zeros, pipeline_mode=pl.Buffered(1))


def _params(sem):
    return pltpu.CompilerParams(dimension_semantics=sem, vmem_limit_bytes=VMEM_LIMIT)


def _ffn_kernel(x_ref, gpre_ref, gpost_ref, wi_ref, wo_ref, o_ref, act_ref):
    sub = x_ref.shape[0] // FFN_SUB_BLOCKS
    for s in range(FFN_SUB_BLOCKS):
        rows = slice(s * sub, (s + 1) * sub)
        h = _rms(x_ref[rows, :], gpre_ref[...]).astype(BF16)
        for f in range(N_FF_TILES):
            gate = jnp.dot(h, wi_ref[:, f * FF_TILE:(f + 1) * FF_TILE], preferred_element_type=F32)
            up = jnp.dot(h, wi_ref[:, D_FF + f * FF_TILE:D_FF + (f + 1) * FF_TILE], preferred_element_type=F32)
            act_ref[rows, f * FF_TILE:(f + 1) * FF_TILE] = (gate * jax.nn.sigmoid(gate) * up).astype(BF16)
    for s in range(FFN_SUB_BLOCKS):
        rows = slice(s * sub, (s + 1) * sub)
        y = jnp.dot(act_ref[rows, :], wo_ref[...], preferred_element_type=F32)
        o_ref[rows, :] = x_ref[rows, :] + 0.5 * _rms(y, gpost_ref[...])


def _ffn(x, g_pre, g_post, wi, wo):
    n = x.shape[0]
    tm = FFN_TOKEN_TILE
    row = pl.BlockSpec((tm, D_MODEL), lambda i: (i, 0))
    return pl.pallas_call(
        _ffn_kernel,
        grid=(n // tm,),
        in_specs=[row, _const_spec((1, D_MODEL)), _const_spec((1, D_MODEL)),
                  _const_spec(wi.shape), _const_spec(wo.shape)],
        out_specs=row,
        out_shape=jax.ShapeDtypeStruct((n, D_MODEL), F32),
        scratch_shapes=[pltpu.VMEM((tm, D_FF), BF16)],
        compiler_params=_params(("parallel",)),
        name="ffn",
    )(x, g_pre, g_post, wi, wo)


def _attn_proj_kernel(x_ref, g_ref, w_ref, c_ref, s1_ref, s2_ref, q_ref, k_ref, v_ref):
    h = _rms(x_ref[...], g_ref[...]).astype(BF16)
    qkv = jnp.dot(h, w_ref[...], preferred_element_type=F32)
    c, s1, s2 = c_ref[...], s1_ref[...], s2_ref[...]
    n_q = N_HEADS * HEAD_DIM // LANES
    n_k = N_KV_HEADS
    for j in range(n_q + n_k):
        xj = qkv[:, j * LANES:(j + 1) * LANES]
        rot = xj * c + pltpu.roll(xj, LANES - ROT_DIM // 2, 1) * s1 + pltpu.roll(xj, ROT_DIM // 2, 1) * s2
        if j < n_q:
            q_ref[:, j * LANES:(j + 1) * LANES] = (rot * ATT_Q_SCALE).astype(BF16)
        else:
            jk = j - n_q
            k_ref[:, jk * LANES:(jk + 1) * LANES] = rot.astype(BF16)
    v_ref[...] = qkv[:, (n_q + n_k) * LANES:].astype(BF16)


def _attn_proj(x, g, w, tabs, seq_len):
    n = x.shape[0]
    tm = TOKEN_TILE
    per_seq = seq_len // tm
    row = pl.BlockSpec((tm, D_MODEL), lambda i: (i, 0))
    tab = pl.BlockSpec((tm, LANES), lambda i: (i % per_seq, 0))
    kv_w = N_KV_HEADS * LANES
    return pl.pallas_call(
        _attn_proj_kernel,
        grid=(n // tm,),
        in_specs=[row, _const_spec((1, D_MODEL)), _const_spec(w.shape), tab, tab, tab],
        out_specs=[row, pl.BlockSpec((tm, kv_w), lambda i: (i, 0)), pl.BlockSpec((tm, kv_w), lambda i: (i, 0))],
        out_shape=[jax.ShapeDtypeStruct((n, D_MODEL), BF16), jax.ShapeDtypeStruct((n, kv_w), BF16),
                   jax.ShapeDtypeStruct((n, kv_w), BF16)],
        compiler_params=_params(("parallel",)),
        name="attn_proj",
    )(x, g, w, *tabs)


def _attn_core_kernel(sink_ref, q_ref, kp_ref, km_ref, kn_ref, vp_ref, vm_ref, vn_ref, x_ref, g_ref, wo_ref,
                      o_ref, kext_ref, vt_ref, obuf_ref, bias_ref, *, n_blocks):
    tq = q_ref.shape[0]
    blocks = tq // ATT_BLOCK
    i = pl.program_id(1)
    blk = ATT_BLOCK
    kext_ref[0:blk] = kp_ref[...]
    kext_ref[blk:blk + tq] = km_ref[...]
    kext_ref[blk + tq:] = kn_ref[...]
    vt_ref[:, 0:blk] = vp_ref[...].T
    vt_ref[:, blk:blk + tq] = vm_ref[...].T
    vt_ref[:, blk + tq:] = vn_ref[...].T

    width = GROUP * blk
    key = lax.broadcasted_iota(jnp.int32, (blk, width), 0)
    qlane = lax.broadcasted_iota(jnp.int32, (blk, width), 1)
    qpos = qlane & (blk - 1)
    neg_inf = jnp.float32(-jnp.inf)
    bias_ref[0] = jnp.where(key >= qpos, 0.0, neg_inf)
    bias_ref[1] = jnp.where(key <= qpos, 0.0, neg_inf)
    lane = lax.broadcasted_iota(jnp.int32, (blk, LANES), 1)
    lo = lane < HEAD_DIM
    top = lax.broadcasted_iota(jnp.int32, (LANES, blk), 0) < HEAD_DIM
    head_of_lane = lax.broadcasted_iota(jnp.int32, (1, width), 1) // blk
    sink_rows = []
    for g in range(N_KV_HEADS):
        row = jnp.full((1, width), sink_ref[g * GROUP], F32)
        for r in range(1, GROUP):
            row = jnp.where(head_of_lane == r, sink_ref[g * GROUP + r], row)
        sink_rows.append(row)

    def scores(qb, g):
        r0 = qb * blk
        kg = kext_ref[r0:r0 + 3 * blk, g * LANES:(g + 1) * LANES]
        parts = []
        for pair in range(GROUP // 2):
            c0 = (g * GROUP // 2 + pair) * LANES
            qp = q_ref[r0:r0 + blk, c0:c0 + LANES]
            zero = jnp.zeros_like(qp)
            parts.append(jnp.where(lo, qp, zero))
            parts.append(jnp.where(lo, zero, qp))
        qs = jnp.concatenate(parts, axis=0)
        return lax.dot_general(kg, qs, (((1,), (1,)), ((), ())), preferred_element_type=F32)

    def finish(qb, g, st):
        gb = i * blocks + qb
        r0 = qb * blk
        s_prev = st[:blk] + (bias_ref[0] + jnp.where(gb == 0, neg_inf, 0.0))
        s_mid = st[blk:2 * blk]
        s_next = st[2 * blk:] + (bias_ref[1] + jnp.where(gb == n_blocks - 1, neg_inf, 0.0))
        m = jnp.max(jnp.maximum(jnp.maximum(s_prev, s_mid), s_next), axis=0, keepdims=True)
        m = jnp.maximum(m, sink_rows[g])
        p_prev, p_mid, p_next = jnp.exp2(s_prev - m), jnp.exp2(s_mid - m), jnp.exp2(s_next - m)
        denom = jnp.sum(p_prev + p_mid + p_next, axis=0, keepdims=True) + jnp.exp2(sink_rows[g] - m)
        pt = jnp.concatenate([p_prev, p_mid, p_next], axis=0).astype(BF16)
        vt = vt_ref[g * LANES:(g + 1) * LANES, r0:r0 + 3 * blk]
        ot = jnp.dot(vt, pt, preferred_element_type=F32) * (1.0 / denom)
        for pair in range(GROUP // 2):
            a = ot[:, (2 * pair) * blk:(2 * pair + 1) * blk]
            b = ot[:, (2 * pair + 1) * blk:(2 * pair + 2) * blk]
            c0 = (g * GROUP // 2 + pair) * LANES
            obuf_ref[r0:r0 + blk, c0:c0 + LANES] = jnp.where(top, a, b).T.astype(BF16)

    tiles = [(qb, g) for qb in range(blocks) for g in range(N_KV_HEADS)]
    st_next = scores(*tiles[0])
    for t, (qb, g) in enumerate(tiles):
        st = st_next
        if t + 1 < len(tiles):
            st_next = scores(*tiles[t + 1])
        finish(qb, g, st)

    y = jnp.dot(obuf_ref[...], wo_ref[...], preferred_element_type=F32)
    o_ref[...] = x_ref[...] + _rms(y, g_ref[...])


def _attn_core(sink, q, kd, vd, x, g, wo, batch, seq_len):
    tq = ATT_Q_TILE
    blocks = tq // ATT_BLOCK
    n_blocks = seq_len // ATT_BLOCK
    kv_w = N_KV_HEADS * LANES
    q3 = q.reshape(batch, seq_len, D_MODEL)
    k3 = kd.reshape(batch, seq_len, kv_w)
    v3 = vd.reshape(batch, seq_len, kv_w)
    x3 = x.reshape(batch, seq_len, D_MODEL)
    row = pl.BlockSpec((None, tq, D_MODEL), lambda b, i: (b, i, 0))
    kv_main = pl.BlockSpec((None, tq, kv_w), lambda b, i: (b, i, 0))
    kv_prev = pl.BlockSpec((None, ATT_BLOCK, kv_w), lambda b, i: (b, jnp.maximum(i * blocks - 1, 0), 0))
    kv_next = pl.BlockSpec((None, ATT_BLOCK, kv_w), lambda b, i: (b, jnp.minimum((i + 1) * blocks, n_blocks - 1), 0))
    out = pl.pallas_call(
        functools.partial(_attn_core_kernel, n_blocks=n_blocks),
        grid=(batch, seq_len // tq),
        in_specs=[pl.BlockSpec(memory_space=pltpu.SMEM), row, kv_prev, kv_main, kv_next, kv_prev, kv_main, kv_next,
                  row, _const_spec((1, D_MODEL)), _const_spec(wo.shape)],
        out_specs=row,
        out_shape=jax.ShapeDtypeStruct((batch, seq_len, D_MODEL), F32),
        scratch_shapes=[pltpu.VMEM((tq + 2 * ATT_BLOCK, kv_w), BF16), pltpu.VMEM((kv_w, tq + 2 * ATT_BLOCK), BF16),
                        pltpu.VMEM((tq, D_MODEL), BF16), pltpu.VMEM((2, ATT_BLOCK, GROUP * ATT_BLOCK), F32)],
        compiler_params=_params(("parallel", "parallel")),
        name="attn_core",
    )(sink, q3, k3, k3, k3, v3, v3, v3, x3, g, wo)
    return out.reshape(batch * seq_len, D_MODEL)


def _ret_proj_kernel(x_ref, g_ref, w_ref, cos_ref, sin_ref, o_ref):
    half = RET_QK_DIM // 2
    n_chunks = RET_IN_DIM // D_MODEL
    sub = x_ref.shape[0] // PROJ_SUB_BLOCKS
    for s in range(PROJ_SUB_BLOCKS):
        rows = slice(s * sub, (s + 1) * sub)
        h = _rms(x_ref[rows, :], g_ref[...]).astype(BF16)
        cos, sin = cos_ref[rows, :], sin_ref[rows, :]
        for c in range(n_chunks):
            y = jnp.dot(h, w_ref[:, c * D_MODEL:(c + 1) * D_MODEL], preferred_element_type=F32)
            if c < 2:
                scale = 1.0 if c == 0 else RET_QK_DIM ** -0.5
                for hd in range(RET_HEADS):
                    x1 = y[:, hd * RET_QK_DIM:hd * RET_QK_DIM + half]
                    x2 = y[:, hd * RET_QK_DIM + half:(hd + 1) * RET_QK_DIM]
                    base = c * D_MODEL + hd * RET_QK_DIM
                    o_ref[rows, base:base + half] = ((x1 * cos - x2 * sin) * scale).astype(BF16)
                    o_ref[rows, base + half:base + 2 * half] = ((x2 * cos + x1 * sin) * scale).astype(BF16)
            elif c < 4:
                o_ref[rows, c * D_MODEL:(c + 1) * D_MODEL] = y.astype(BF16)
            else:
                o_ref[rows, c * D_MODEL:(c + 1) * D_MODEL] = (y * jax.nn.sigmoid(y)).astype(BF16)


def _ret_proj(x, g, w, tabs, seq_len):
    n = x.shape[0]
    tm = TOKEN_TILE
    per_seq = seq_len // tm
    row = pl.BlockSpec((tm, D_MODEL), lambda i: (i, 0))
    tab = pl.BlockSpec((tm, LANES), lambda i: (i % per_seq, 0))
    return pl.pallas_call(
        _ret_proj_kernel,
        grid=(n // tm,),
        in_specs=[row, _const_spec((1, D_MODEL)), _const_spec(w.shape), tab, tab],
        out_specs=pl.BlockSpec((tm, RET_IN_DIM), lambda i: (i, 0)),
        out_shape=jax.ShapeDtypeStruct((n, RET_IN_DIM), BF16),
        compiler_params=_params(("parallel",)),
        name="ret_proj",
    )(x, g, w, *tabs)


def _ret_core_kernel(lgf_ref, lgb_ref, q_ref, k_ref, v_ref, o_ref, acc_ref, st_ref, dmat_ref):
    blk = RET_BLOCK
    seq_len = q_ref.shape[0]
    n_chunks = seq_len // blk
    hd = pl.program_id(1)
    lgf = lgf_ref[hd]
    lgb = lgb_ref[hd]
    ii = lax.broadcasted_iota(jnp.int32, (blk, blk), 0)
    jj = lax.broadcasted_iota(jnp.int32, (blk, blk), 1)
    diff = (ii - jj).astype(F32)
    dmat_ref[...] = jnp.where(diff >= 0, jnp.exp(lgf * jnp.maximum(diff, 0.0)),
                              jnp.exp(lgb * jnp.maximum(-diff, 0.0)))
    idx = lax.broadcasted_iota(jnp.int32, (blk, 1), 0).astype(F32)
    qdec_f = jnp.exp(lgf * (idx + 1.0))
    kdec_f = jnp.exp(lgf * (blk - 1.0 - idx))
    cdec_f = jnp.exp(lgf * blk)
    qdec_b = jnp.exp(lgb * (blk - idx))
    kdec_b = jnp.exp(lgb * idx)
    cdec_b = jnp.exp(lgb * blk)

    def state_delta(kc, vc, kdec):
        kd = (kc.astype(F32) * kdec).astype(BF16)
        return lax.dot_general(kd, vc, (((0,), (0,)), ((), ())), preferred_element_type=F32)

    st_ref[...] = jnp.zeros_like(st_ref)

    def fwd(c, carry):
        rows = pl.ds(pl.multiple_of(c * blk, blk), blk)
        qc, kc, vc = q_ref[rows, :], k_ref[rows, :], v_ref[rows, :]
        s = lax.dot_general(qc, kc, (((1,), (1,)), ((), ())), preferred_element_type=F32)
        delta = state_delta(kc, vc, kdec_f)
        cross = jnp.dot(qc, st_ref[...].astype(BF16), preferred_element_type=F32) * qdec_f
        intra = jnp.dot((s * dmat_ref[...]).astype(BF16), vc, preferred_element_type=F32)
        acc_ref[rows, :] = intra + cross
        st_ref[...] = cdec_f * st_ref[...] + delta
        return carry

    lax.fori_loop(0, n_chunks, fwd, 0, unroll=RET_UNROLL)

    st_ref[...] = jnp.zeros_like(st_ref)

    def bwd(t, carry):
        c = n_chunks - 1 - t
        rows = pl.ds(pl.multiple_of(c * blk, blk), blk)
        qc, kc, vc = q_ref[rows, :], k_ref[rows, :], v_ref[rows, :]
        delta = state_delta(kc, vc, kdec_b)
        cross = jnp.dot(qc, st_ref[...].astype(BF16), preferred_element_type=F32) * qdec_b
        o_ref[rows, :] = (acc_ref[rows, :] + cross).astype(BF16)
        st_ref[...] = cdec_b * st_ref[...] + delta
        return carry

    lax.fori_loop(0, n_chunks, bwd, 0, unroll=RET_UNROLL)


def _ret_core(lgf, lgb, proj, batch, seq_len):
    p3 = proj.reshape(batch, seq_len, RET_IN_DIM)
    qk_blocks = D_MODEL // RET_QK_DIM
    v_off = 2 * D_MODEL // RET_V_DIM
    smem = pl.BlockSpec(memory_space=pltpu.SMEM)
    out = pl.pallas_call(
        _ret_core_kernel,
        grid=(batch, RET_HEADS),
        in_specs=[smem, smem,
                  pl.BlockSpec((None, seq_len, RET_QK_DIM), lambda b, h: (b, 0, h)),
                  pl.BlockSpec((None, seq_len, RET_QK_DIM), lambda b, h: (b, 0, qk_blocks + h)),
                  pl.BlockSpec((None, seq_len, RET_V_DIM), lambda b, h: (b, 0, v_off + h))],
        out_specs=pl.BlockSpec((None, seq_len, RET_V_DIM), lambda b, h: (b, 0, h)),
        out_shape=jax.ShapeDtypeStruct((batch, seq_len, RET_V_TOTAL), BF16),
        scratch_shapes=[pltpu.VMEM((seq_len, RET_V_DIM), F32), pltpu.VMEM((RET_QK_DIM, RET_V_DIM), F32),
                        pltpu.VMEM((RET_BLOCK, RET_BLOCK), F32)],
        compiler_params=_params(("parallel", "parallel")),
        name="ret_core",
    )(lgf, lgb, p3, p3, p3)
    return out.reshape(batch * seq_len, RET_V_TOTAL)


def _ret_out_kernel(y_ref, gate_ref, x_ref, g_ref, w_ref, o_ref, z_ref):
    sub = x_ref.shape[0] // PROJ_SUB_BLOCKS
    for s in range(PROJ_SUB_BLOCKS):
        rows = slice(s * sub, (s + 1) * sub)
        for hd in range(RET_HEADS):
            cols = slice(hd * RET_V_DIM, (hd + 1) * RET_V_DIM)
            y = y_ref[rows, cols].astype(F32)
            yn = y * lax.rsqrt(jnp.mean(y * y, axis=-1, keepdims=True) + NORM_EPS)
            z_ref[rows, cols] = (gate_ref[rows, cols].astype(F32) * yn).astype(BF16)
        out = jnp.dot(z_ref[rows, :], w_ref[...], preferred_element_type=F32)
        o_ref[rows, :] = x_ref[rows, :] + _rms(out, g_ref[...])


def _ret_out(y, proj, x, g, w):
    n = x.shape[0]
    tm = TOKEN_TILE
    row = pl.BlockSpec((tm, D_MODEL), lambda i: (i, 0))
    wide = pl.BlockSpec((tm, RET_V_TOTAL), lambda i: (i, 0))
    gate_cols = pl.BlockSpec((tm, RET_V_TOTAL), lambda i: (i, RET_IN_DIM // RET_V_TOTAL - 1))
    return pl.pallas_call(
        _ret_out_kernel,
        grid=(n // tm,),
        in_specs=[wide, gate_cols, row, _const_spec((1, D_MODEL)), _const_spec(w.shape)],
        out_specs=row,
        out_shape=jax.ShapeDtypeStruct((n, D_MODEL), F32),
        scratch_shapes=[pltpu.VMEM((tm, RET_V_TOTAL), BF16)],
        compiler_params=_params(("parallel",)),
        name="ret_out",
    )(y, proj, x, g, w)


def _attn_tables(seq_len):
    half = ROT_DIM // 2
    inv_freq = ROPE_THETA ** (-jnp.arange(half, dtype=F32) / half)
    ang = jnp.arange(seq_len, dtype=F32)[:, None] * inv_freq[None, :]
    cos, sin = jnp.cos(ang), jnp.sin(ang)
    pad = HEAD_DIM - ROT_DIM
    ones = jnp.ones((seq_len, pad), F32)
    zeros = jnp.zeros((seq_len, pad), F32)
    zh = jnp.zeros((seq_len, half), F32)
    c = jnp.concatenate([cos, cos, ones], axis=1)
    s1 = jnp.concatenate([-sin, zh, zeros], axis=1)
    s2 = jnp.concatenate([zh, sin, zeros], axis=1)
    reps = LANES // HEAD_DIM
    return tuple(jnp.tile(t, (1, reps)) for t in (c, s1, s2))


def _ret_tables(seq_len):
    half = RET_QK_DIM // 2
    inv_freq = RET_THETA ** (-jnp.arange(half, dtype=F32) / half)
    ang = jnp.arange(seq_len, dtype=F32)[:, None] * inv_freq[None, :]
    return jnp.cos(ang), jnp.sin(ang)


def _attn_weights(w_qkv):
    q_dim = N_HEADS * HEAD_DIM
    kv_dim = N_KV_HEADS * HEAD_DIM
    wq = w_qkv[:, :q_dim]
    wk = w_qkv[:, q_dim:q_dim + kv_dim].reshape(D_MODEL, N_KV_HEADS, 1, HEAD_DIM)
    wv = w_qkv[:, q_dim + kv_dim:].reshape(D_MODEL, N_KV_HEADS, 1, HEAD_DIM)
    reps = LANES // HEAD_DIM
    wk = jnp.broadcast_to(wk, (D_MODEL, N_KV_HEADS, reps, HEAD_DIM)).reshape(D_MODEL, N_KV_HEADS * LANES)
    wv = jnp.broadcast_to(wv, (D_MODEL, N_KV_HEADS, reps, HEAD_DIM)).reshape(D_MODEL, N_KV_HEADS * LANES)
    return jnp.concatenate([wq, wk, wv], axis=1).astype(BF16)


def _trunk(x, gains, ffn_w, attn_w, attn_wo, sink, ret_w, ret_wo, lgf, lgb, attn_tabs, ret_tabs):
    batch, seq_len, _ = x.shape
    x = x.reshape(batch * seq_len, D_MODEL)
    for layer in range(2):
        g = gains[layer]
        x = _ffn(x, g[0], g[1], *ffn_w[layer][0])
        if layer == 0:
            q, kd, vd = _attn_proj(x, g[2], attn_w, attn_tabs, seq_len)
            x = _attn_core(sink, q, kd, vd, x, g[3], attn_wo, batch, seq_len)
        else:
            proj = _ret_proj(x, g[2], ret_w, ret_tabs, seq_len)
            y = _ret_core(lgf, lgb, proj, batch, seq_len)
            x = _ret_out(y, proj, x, g[3], ret_wo)
        x = _ffn(x, g[4], g[5], *ffn_w[layer][1])
    return x.reshape(batch, seq_len, D_MODEL)


def kernel(x_prompt, x_sample, norm_gains, ffn_w_in, ffn_w_out, attn_w_qkv, attn_w_o, attn_sink, ret_w_in, ret_w_o, ret_decay_fwd, ret_decay_bwd):
    gains = norm_gains.astype(F32).reshape(2, 6, 1, D_MODEL)
    ffn_w = [[(ffn_w_in[l, s].astype(BF16), ffn_w_out[l, s].astype(BF16)) for s in range(2)] for l in range(2)]
    attn_w = _attn_weights(attn_w_qkv[0])
    attn_wo = attn_w_o[0].astype(BF16)
    sink = attn_sink[0].astype(F32) * LOG2_E
    ret_w = ret_w_in[0].astype(BF16)
    ret_wo = ret_w_o[0].astype(BF16)
    lgf = jax.nn.log_sigmoid(ret_decay_fwd[0].astype(F32))
    lgb = jax.nn.log_sigmoid(ret_decay_bwd[0].astype(F32))
    outs = []
    for x in (x_prompt, x_sample):
        seq_len = x.shape[1]
        outs.append(_trunk(x, gains, ffn_w, attn_w, attn_wo, sink, ret_w, ret_wo, lgf, lgb,
                           _attn_tables(seq_len), _ret_tables(seq_len)))
    return tuple(outs)
```

```python
import functools

import jax
import jax.numpy as jnp
import numpy as np
from jax import lax
from jax.experimental import pallas as pl
from jax.experimental.pallas import tpu as pltpu

D_MODEL = 1024
HEAD_DIM = 64
N_HEADS = 16
N_KV_HEADS = 4
GROUP = 4
ATT_BLOCK = 128
ROT_DIM = 16
ROPE_THETA = 500000.0
RET_HEADS = 4
RET_QK_DIM = 256
RET_V_DIM = 512
RET_V_TOTAL = 2048
RET_IN_DIM = 6144
RET_THETA = 10000.0
D_FF = 2816
NORM_EPS = 1e-6

LANES = 128
FF_TILE = 256
N_FF_TILES = D_FF // FF_TILE
TOKEN_TILE = 512
PROJ_SUB_BLOCKS = 2
FFN_TOKEN_TILE = 1024
FFN_SUB_BLOCKS = 4
ATT_Q_TILE = 512
ATT_LOOKAHEAD = 3
RET_BLOCK = 256
RET_UNROLL = 8
VMEM_LIMIT = 56 * 1024 * 1024

LOG2_E = 1.4426950408889634
ATT_Q_SCALE = HEAD_DIM ** -0.5 * LOG2_E

F32 = jnp.float32
BF16 = jnp.bfloat16


def _rms(x, g):
    return x * lax.rsqrt(jnp.mean(x * x, axis=-1, keepdims=True) + NORM_EPS) * g


def _const_spec(shape):
    zeros = (0,) * len(shape)
    return pl.BlockSpec(shape, lambda *_: zeros, pipeline_mode=pl.Buffered(1))


def _params(sem):
    return pltpu.CompilerParams(dimension_semantics=sem, vmem_limit_bytes=VMEM_LIMIT)


def _ffn_kernel(x_ref, gpre_ref, gpost_ref, wi_ref, wo_ref, o_ref, act_ref):
    sub = x_ref.shape[0] // FFN_SUB_BLOCKS
    for s in range(FFN_SUB_BLOCKS):
        rows = slice(s * sub, (s + 1) * sub)
        h = _rms(x_ref[rows, :], gpre_ref[...]).astype(BF16)
        for f in range(N_FF_TILES):
            gate = jnp.dot(h, wi_ref[:, f * FF_TILE:(f + 1) * FF_TILE], preferred_element_type=F32)
            up = jnp.dot(h, wi_ref[:, D_FF + f * FF_TILE:D_FF + (f + 1) * FF_TILE], preferred_element_type=F32)
            act_ref[rows, f * FF_TILE:(f + 1) * FF_TILE] = (gate * jax.nn.sigmoid(gate) * up).astype(BF16)
    for s in range(FFN_SUB_BLOCKS):
        rows = slice(s * sub, (s + 1) * sub)
        y = jnp.dot(act_ref[rows, :], wo_ref[...], preferred_element_type=F32)
        o_ref[rows, :] = x_ref[rows, :] + 0.5 * _rms(y, gpost_ref[...])


def _slab_spec(stacked, layer, slot):
    return pl.BlockSpec((None, None) + stacked.shape[2:], lambda *_: (layer, slot, 0, 0),
                        pipeline_mode=pl.Buffered(1))


def _ffn(x, g_pre, g_post, wi, wo, layer, slot):
    n = x.shape[0]
    tm = FFN_TOKEN_TILE
    row = pl.BlockSpec((tm, D_MODEL), lambda i: (i, 0))
    return pl.pallas_call(
        _ffn_kernel,
        grid=(n // tm,),
        in_specs=[row, _const_spec((1, D_MODEL)), _const_spec((1, D_MODEL)),
                  _slab_spec(wi, layer, slot), _slab_spec(wo, layer, slot)],
        out_specs=row,
        out_shape=jax.ShapeDtypeStruct((n, D_MODEL), F32),
        scratch_shapes=[pltpu.VMEM((tm, D_FF), BF16)],
        compiler_params=_params(("parallel",)),
        name="ffn",
    )(x, g_pre, g_post, wi, wo)


def _attn_proj_kernel(x_ref, g_ref, w_ref, c_ref, s1_ref, s2_ref, q_ref, k_ref, v_ref):
    h = _rms(x_ref[...], g_ref[...]).astype(BF16)
    qkv = jnp.dot(h, w_ref[...], preferred_element_type=F32)
    c, s1, s2 = c_ref[...], s1_ref[...], s2_ref[...]
    n_q = N_HEADS * HEAD_DIM // LANES
    n_k = N_KV_HEADS
    for j in range(n_q + n_k):
        xj = qkv[:, j * LANES:(j + 1) * LANES]
        rot = xj * c + pltpu.roll(xj, LANES - ROT_DIM // 2, 1) * s1 + pltpu.roll(xj, ROT_DIM // 2, 1) * s2
        if j < n_q:
            q_ref[:, j * LANES:(j + 1) * LANES] = (rot * ATT_Q_SCALE).astype(BF16)
        else:
            jk = j - n_q
            k_ref[:, jk * LANES:(jk + 1) * LANES] = rot.astype(BF16)
    v_ref[...] = qkv[:, (n_q + n_k) * LANES:].astype(BF16)


def _attn_proj(x, g, w, tabs, seq_len):
    n = x.shape[0]
    tm = TOKEN_TILE
    per_seq = seq_len // tm
    row = pl.BlockSpec((tm, D_MODEL), lambda i: (i, 0))
    tab = pl.BlockSpec((tm, LANES), lambda i: (i % per_seq, 0))
    kv_w = N_KV_HEADS * LANES
    return pl.pallas_call(
        _attn_proj_kernel,
        grid=(n // tm,),
        in_specs=[row, _const_spec((1, D_MODEL)), _const_spec(w.shape), tab, tab, tab],
        out_specs=[row, pl.BlockSpec((tm, kv_w), lambda i: (i, 0)), pl.BlockSpec((tm, kv_w), lambda i: (i, 0))],
        out_shape=[jax.ShapeDtypeStruct((n, D_MODEL), BF16), jax.ShapeDtypeStruct((n, kv_w), BF16),
                   jax.ShapeDtypeStruct((n, kv_w), BF16)],
        compiler_params=_params(("parallel",)),
        name="attn_proj",
    )(x, g, w, *tabs)


def _attn_core_kernel(sink_ref, q_ref, kp_ref, km_ref, kn_ref, vp_ref, vm_ref, vn_ref, x_ref, g_ref, wo_ref,
                      o_ref, kext_ref, vt_ref, obuf_ref, bias_ref, *, n_blocks):
    tq = q_ref.shape[0]
    blocks = tq // ATT_BLOCK
    i = pl.program_id(1)
    blk = ATT_BLOCK
    kext_ref[0:blk] = kp_ref[...]
    kext_ref[blk:blk + tq] = km_ref[...]
    kext_ref[blk + tq:] = kn_ref[...]
    vt_ref[:, 0:blk] = vp_ref[...].T
    vt_ref[:, blk:blk + tq] = vm_ref[...].T
    vt_ref[:, blk + tq:] = vn_ref[...].T

    heads_per_col = LANES // HEAD_DIM
    width = heads_per_col * blk
    n_cols = N_HEADS // heads_per_col
    key = lax.broadcasted_iota(jnp.int32, (blk, width), 0)
    qlane = lax.broadcasted_iota(jnp.int32, (blk, width), 1)
    qpos = qlane & (blk - 1)
    neg_inf = jnp.float32(-jnp.inf)
    bias_ref[0] = jnp.where(key >= qpos, 0.0, neg_inf)
    bias_ref[1] = jnp.where(key <= qpos, 0.0, neg_inf)
    lane = lax.broadcasted_iota(jnp.int32, (blk, LANES), 1)
    lo = lane < HEAD_DIM
    top = lax.broadcasted_iota(jnp.int32, (LANES, blk), 0) < HEAD_DIM
    first_head = lax.broadcasted_iota(jnp.int32, (1, width), 1) < blk
    sink_rows = [jnp.where(first_head, sink_ref[heads_per_col * c], sink_ref[heads_per_col * c + 1])
                 for c in range(n_cols)]

    def scores(qb, c):
        r0 = qb * blk
        g = c * heads_per_col // GROUP
        kg = kext_ref[r0:r0 + 3 * blk, g * LANES:(g + 1) * LANES]
        qp = q_ref[r0:r0 + blk, c * LANES:(c + 1) * LANES]
        zero = jnp.zeros_like(qp)
        qs = jnp.concatenate([jnp.where(lo, qp, zero), jnp.where(lo, zero, qp)], axis=0)
        return lax.dot_general(kg, qs, (((1,), (1,)), ((), ())), preferred_element_type=F32)

    def finish(qb, c, st):
        gb = i * blocks + qb
        r0 = qb * blk
        g = c * heads_per_col // GROUP
        s_prev = st[:blk] + (bias_ref[0] + jnp.where(gb == 0, neg_inf, 0.0))
        s_mid = st[blk:2 * blk]
        s_next = st[2 * blk:] + (bias_ref[1] + jnp.where(gb == n_blocks - 1, neg_inf, 0.0))
        m = jnp.max(jnp.maximum(jnp.maximum(s_prev, s_mid), s_next), axis=0, keepdims=True)
        m = jnp.maximum(m, sink_rows[c])
        p_prev, p_mid, p_next = jnp.exp2(s_prev - m), jnp.exp2(s_mid - m), jnp.exp2(s_next - m)
        denom = jnp.sum(p_prev + p_mid + p_next, axis=0, keepdims=True) + jnp.exp2(sink_rows[c] - m)
        pt = jnp.concatenate([p_prev, p_mid, p_next], axis=0).astype(BF16)
        vt = vt_ref[g * LANES:(g + 1) * LANES, r0:r0 + 3 * blk]
        ot = jnp.dot(vt, pt, preferred_element_type=F32) * (1.0 / denom)
        obuf_ref[r0:r0 + blk, c * LANES:(c + 1) * LANES] = jnp.where(top, ot[:, :blk], ot[:, blk:]).T.astype(BF16)

    tiles = [(qb, c) for qb in range(blocks) for c in range(n_cols)]
    pending = [scores(*tile) for tile in tiles[:ATT_LOOKAHEAD]]
    for t, (qb, c) in enumerate(tiles):
        if t + ATT_LOOKAHEAD < len(tiles):
            pending.append(scores(*tiles[t + ATT_LOOKAHEAD]))
        finish(qb, c, pending.pop(0))

    y = jnp.dot(obuf_ref[...], wo_ref[...], preferred_element_type=F32)
    o_ref[...] = x_ref[...] + _rms(y, g_ref[...])


def _attn_core(sink, q, kd, vd, x, g, wo, batch, seq_len):
    tq = ATT_Q_TILE
    blocks = tq // ATT_BLOCK
    n_blocks = seq_len // ATT_BLOCK
    kv_w = N_KV_HEADS * LANES
    q3 = q.reshape(batch, seq_len, D_MODEL)
    k3 = kd.reshape(batch, seq_len, kv_w)
    v3 = vd.reshape(batch, seq_len, kv_w)
    x3 = x.reshape(batch, seq_len, D_MODEL)
    row = pl.BlockSpec((None, tq, D_MODEL), lambda b, i: (b, i, 0))
    kv_main = pl.BlockSpec((None, tq, kv_w), lambda b, i: (b, i, 0))
    kv_prev = pl.BlockSpec((None, ATT_BLOCK, kv_w), lambda b, i: (b, jnp.maximum(i * blocks - 1, 0), 0))
    kv_next = pl.BlockSpec((None, ATT_BLOCK, kv_w), lambda b, i: (b, jnp.minimum((i + 1) * blocks, n_blocks - 1), 0))
    out = pl.pallas_call(
        functools.partial(_attn_core_kernel, n_blocks=n_blocks),
        grid=(batch, seq_len // tq),
        in_specs=[pl.BlockSpec(memory_space=pltpu.SMEM), row, kv_prev, kv_main, kv_next, kv_prev, kv_main, kv_next,
                  row, _const_spec((1, D_MODEL)), _const_spec(wo.shape)],
        out_specs=row,
        out_shape=jax.ShapeDtypeStruct((batch, seq_len, D_MODEL), F32),
        scratch_shapes=[pltpu.VMEM((tq + 2 * ATT_BLOCK, kv_w), BF16), pltpu.VMEM((kv_w, tq + 2 * ATT_BLOCK), BF16),
                        pltpu.VMEM((tq, D_MODEL), BF16),
                        pltpu.VMEM((2, ATT_BLOCK, LANES // HEAD_DIM * ATT_BLOCK), F32)],
        compiler_params=_params(("parallel", "parallel")),
        name="attn_core",
    )(sink, q3, k3, k3, k3, v3, v3, v3, x3, g, wo)
    return out.reshape(batch * seq_len, D_MODEL)


def _ret_proj_kernel(x_ref, g_ref, w_ref, cos_ref, sin_ref, o_ref):
    half = RET_QK_DIM // 2
    n_chunks = RET_IN_DIM // D_MODEL
    sub = x_ref.shape[0] // PROJ_SUB_BLOCKS
    for s in range(PROJ_SUB_BLOCKS):
        rows = slice(s * sub, (s + 1) * sub)
        h = _rms(x_ref[rows, :], g_ref[...]).astype(BF16)
        cos, sin = cos_ref[rows, :], sin_ref[rows, :]
        for c in range(n_chunks):
            y = jnp.dot(h, w_ref[:, c * D_MODEL:(c + 1) * D_MODEL], preferred_element_type=F32)
            if c < 2:
                scale = 1.0 if c == 0 else RET_QK_DIM ** -0.5
                for hd in range(RET_HEADS):
                    x1 = y[:, hd * RET_QK_DIM:hd * RET_QK_DIM + half]
                    x2 = y[:, hd * RET_QK_DIM + half:(hd + 1) * RET_QK_DIM]
                    base = c * D_MODEL + hd * RET_QK_DIM
                    o_ref[rows, base:base + half] = ((x1 * cos - x2 * sin) * scale).astype(BF16)
                    o_ref[rows, base + half:base + 2 * half] = ((x2 * cos + x1 * sin) * scale).astype(BF16)
            elif c < 4:
                o_ref[rows, c * D_MODEL:(c + 1) * D_MODEL] = y.astype(BF16)
            else:
                o_ref[rows, c * D_MODEL:(c + 1) * D_MODEL] = (y * jax.nn.sigmoid(y)).astype(BF16)


def _ret_proj(x, g, w, tabs, seq_len):
    n = x.shape[0]
    tm = TOKEN_TILE
    per_seq = seq_len // tm
    row = pl.BlockSpec((tm, D_MODEL), lambda i: (i, 0))
    tab = pl.BlockSpec((tm, LANES), lambda i: (i % per_seq, 0))
    return pl.pallas_call(
        _ret_proj_kernel,
        grid=(n // tm,),
        in_specs=[row, _const_spec((1, D_MODEL)), _const_spec(w.shape), tab, tab],
        out_specs=pl.BlockSpec((tm, RET_IN_DIM), lambda i: (i, 0)),
        out_shape=jax.ShapeDtypeStruct((n, RET_IN_DIM), BF16),
        compiler_params=_params(("parallel",)),
        name="ret_proj",
    )(x, g, w, *tabs)


def _ret_core_kernel(lgf_ref, lgb_ref, q_ref, k_ref, v_ref, o_ref, acc_ref, st_ref, dmat_ref):
    blk = RET_BLOCK
    seq_len = q_ref.shape[0]
    n_chunks = seq_len // blk
    hd = pl.program_id(1)
    lgf = lgf_ref[hd]
    lgb = lgb_ref[hd]
    ii = lax.broadcasted_iota(jnp.int32, (blk, blk), 0)
    jj = lax.broadcasted_iota(jnp.int32, (blk, blk), 1)
    diff = (ii - jj).astype(F32)
    dmat_ref[...] = jnp.where(diff >= 0, jnp.exp(lgf * jnp.maximum(diff, 0.0)),
                              jnp.exp(lgb * jnp.maximum(-diff, 0.0)))
    idx = lax.broadcasted_iota(jnp.int32, (blk, 1), 0).astype(F32)
    qdec_f = jnp.exp(lgf * (idx + 1.0))
    kdec_f = jnp.exp(lgf * (blk - 1.0 - idx))
    cdec_f = jnp.exp(lgf * blk)
    qdec_b = jnp.exp(lgb * (blk - idx))
    kdec_b = jnp.exp(lgb * idx)
    cdec_b = jnp.exp(lgb * blk)

    def state_delta(kc, vc, kdec):
        kd = (kc.astype(F32) * kdec).astype(BF16)
        return lax.dot_general(kd, vc, (((0,), (0,)), ((), ())), preferred_element_type=F32)

    st_ref[...] = jnp.zeros_like(st_ref)

    def fwd(c, carry):
        rows = pl.ds(pl.multiple_of(c * blk, blk), blk)
        qc, kc, vc = q_ref[rows, :], k_ref[rows, :], v_ref[rows, :]
        s = lax.dot_general(qc, kc, (((1,), (1,)), ((), ())), preferred_element_type=F32)
        delta = state_delta(kc, vc, kdec_f)
        cross = jnp.dot(qc, st_ref[...].astype(BF16), preferred_element_type=F32) * qdec_f
        intra = jnp.dot((s * dmat_ref[...]).astype(BF16), vc, preferred_element_type=F32)
        acc_ref[rows, :] = intra + cross
        st_ref[...] = cdec_f * st_ref[...] + delta
        return carry

    lax.fori_loop(0, n_chunks, fwd, 0, unroll=RET_UNROLL)

    st_ref[...] = jnp.zeros_like(st_ref)

    def bwd(t, carry):
        c = n_chunks - 1 - t
        rows = pl.ds(pl.multiple_of(c * blk, blk), blk)
        qc, kc, vc = q_ref[rows, :], k_ref[rows, :], v_ref[rows, :]
        delta = state_delta(kc, vc, kdec_b)
        cross = jnp.dot(qc, st_ref[...].astype(BF16), preferred_element_type=F32) * qdec_b
        o_ref[rows, :] = (acc_ref[rows, :] + cross).astype(BF16)
        st_ref[...] = cdec_b * st_ref[...] + delta
        return carry

    lax.fori_loop(0, n_chunks, bwd, 0, unroll=RET_UNROLL)


def _ret_core(lgf, lgb, proj, batch, seq_len):
    p3 = proj.reshape(batch, seq_len, RET_IN_DIM)
    qk_blocks = D_MODEL // RET_QK_DIM
    v_off = 2 * D_MODEL // RET_V_DIM
    smem = pl.BlockSpec(memory_space=pltpu.SMEM)
    out = pl.pallas_call(
        _ret_core_kernel,
        grid=(batch, RET_HEADS),
        in_specs=[smem, smem,
                  pl.BlockSpec((None, seq_len, RET_QK_DIM), lambda b, h: (b, 0, h)),
                  pl.BlockSpec((None, seq_len, RET_QK_DIM), lambda b, h: (b, 0, qk_blocks + h)),
                  pl.BlockSpec((None, seq_len, RET_V_DIM), lambda b, h: (b, 0, v_off + h))],
        out_specs=pl.BlockSpec((None, seq_len, RET_V_DIM), lambda b, h: (b, 0, h)),
        out_shape=jax.ShapeDtypeStruct((batch, seq_len, RET_V_TOTAL), BF16),
        scratch_shapes=[pltpu.VMEM((seq_len, RET_V_DIM), F32), pltpu.VMEM((RET_QK_DIM, RET_V_DIM), F32),
                        pltpu.VMEM((RET_BLOCK, RET_BLOCK), F32)],
        compiler_params=_params(("parallel", "parallel")),
        name="ret_core",
    )(lgf, lgb, p3, p3, p3)
    return out.reshape(batch * seq_len, RET_V_TOTAL)


def _ret_out_kernel(y_ref, gate_ref, x_ref, g_ref, w_ref, o_ref, z_ref):
    sub = x_ref.shape[0] // PROJ_SUB_BLOCKS
    for s in range(PROJ_SUB_BLOCKS):
        rows = slice(s * sub, (s + 1) * sub)
        for hd in range(RET_HEADS):
            cols = slice(hd * RET_V_DIM, (hd + 1) * RET_V_DIM)
            y = y_ref[rows, cols].astype(F32)
            yn = y * lax.rsqrt(jnp.mean(y * y, axis=-1, keepdims=True) + NORM_EPS)
            z_ref[rows, cols] = (gate_ref[rows, cols].astype(F32) * yn).astype(BF16)
        out = jnp.dot(z_ref[rows, :], w_ref[...], preferred_element_type=F32)
        o_ref[rows, :] = x_ref[rows, :] + _rms(out, g_ref[...])


def _ret_out(y, proj, x, g, w):
    n = x.shape[0]
    tm = TOKEN_TILE
    row = pl.BlockSpec((tm, D_MODEL), lambda i: (i, 0))
    wide = pl.BlockSpec((tm, RET_V_TOTAL), lambda i: (i, 0))
    gate_cols = pl.BlockSpec((tm, RET_V_TOTAL), lambda i: (i, RET_IN_DIM // RET_V_TOTAL - 1))
    return pl.pallas_call(
        _ret_out_kernel,
        grid=(n // tm,),
        in_specs=[wide, gate_cols, row, _const_spec((1, D_MODEL)), _const_spec(w.shape)],
        out_specs=row,
        out_shape=jax.ShapeDtypeStruct((n, D_MODEL), F32),
        scratch_shapes=[pltpu.VMEM((tm, RET_V_TOTAL), BF16)],
        compiler_params=_params(("parallel",)),
        name="ret_out",
    )(y, proj, x, g, w)


def _angles(seq_len, half, theta):
    inv_freq = theta ** (-np.arange(half, dtype=np.float64) / half)
    return np.arange(seq_len, dtype=np.float64)[:, None] * inv_freq[None, :]


def _attn_tables(seq_len):
    half = ROT_DIM // 2
    ang = _angles(seq_len, half, ROPE_THETA)
    cos, sin = np.cos(ang), np.sin(ang)
    pad = HEAD_DIM - ROT_DIM
    ones = np.ones((seq_len, pad))
    zeros = np.zeros((seq_len, pad))
    zh = np.zeros((seq_len, half))
    c = np.concatenate([cos, cos, ones], axis=1)
    s1 = np.concatenate([-sin, zh, zeros], axis=1)
    s2 = np.concatenate([zh, sin, zeros], axis=1)
    reps = LANES // HEAD_DIM
    return tuple(jnp.asarray(np.tile(t, (1, reps)), F32) for t in (c, s1, s2))


def _ret_tables(seq_len):
    ang = _angles(seq_len, RET_QK_DIM // 2, RET_THETA)
    return jnp.asarray(np.cos(ang), F32), jnp.asarray(np.sin(ang), F32)


def _attn_weights(w_qkv):
    q_dim = N_HEADS * HEAD_DIM
    kv_dim = N_KV_HEADS * HEAD_DIM
    wq = w_qkv[:, :q_dim]
    wk = w_qkv[:, q_dim:q_dim + kv_dim].reshape(D_MODEL, N_KV_HEADS, 1, HEAD_DIM)
    wv = w_qkv[:, q_dim + kv_dim:].reshape(D_MODEL, N_KV_HEADS, 1, HEAD_DIM)
    reps = LANES // HEAD_DIM
    wk = jnp.broadcast_to(wk, (D_MODEL, N_KV_HEADS, reps, HEAD_DIM)).reshape(D_MODEL, N_KV_HEADS * LANES)
    wv = jnp.broadcast_to(wv, (D_MODEL, N_KV_HEADS, reps, HEAD_DIM)).reshape(D_MODEL, N_KV_HEADS * LANES)
    return jnp.concatenate([wq, wk, wv], axis=1).astype(BF16)


def _trunk(x, gains, ffn_w, attn_w, attn_wo, sink, ret_w, ret_wo, lgf, lgb, attn_tabs, ret_tabs):
    batch, seq_len, _ = x.shape
    x = x.reshape(batch * seq_len, D_MODEL)
    for layer in range(2):
        g = gains[layer]
        x = _ffn(x, g[0], g[1], *ffn_w, layer, 0)
        if layer == 0:
            q, kd, vd = _attn_proj(x, g[2], attn_w, attn_tabs, seq_len)
            x = _attn_core(sink, q, kd, vd, x, g[3], attn_wo, batch, seq_len)
        else:
            proj = _ret_proj(x, g[2], ret_w, ret_tabs, seq_len)
            y = _ret_core(lgf, lgb, proj, batch, seq_len)
            x = _ret_out(y, proj, x, g[3], ret_wo)
        x = _ffn(x, g[4], g[5], *ffn_w, layer, 1)
    return x.reshape(batch, seq_len, D_MODEL)


def kernel(x_prompt, x_sample, norm_gains, ffn_w_in, ffn_w_out, attn_w_qkv, attn_w_o, attn_sink, ret_w_in, ret_w_o, ret_decay_fwd, ret_decay_bwd):
    gains = norm_gains.astype(F32).reshape(2, 6, 1, D_MODEL)
    ffn_w = (ffn_w_in.astype(BF16), ffn_w_out.astype(BF16))
    attn_w = _attn_weights(attn_w_qkv[0])
    attn_wo = attn_w_o[0].astype(BF16)
    sink = attn_sink[0].astype(F32) * LOG2_E
    ret_w = ret_w_in[0].astype(BF16)
    ret_wo = ret_w_o[0].astype(BF16)
    lgf = jax.nn.log_sigmoid(ret_decay_fwd[0].astype(F32))
    lgb = jax.nn.log_sigmoid(ret_decay_bwd[0].astype(F32))
    outs = []
    for x in (x_prompt, x_sample):
        seq_len = x.shape[1]
        outs.append(_trunk(x, gains, ffn_w, attn_w, attn_wo, sink, ret_w, ret_wo, lgf, lgb,
                           _attn_tables(seq_len), _ret_tables(seq_len)))
    return tuple(outs)
```

```python
import functools

import jax
import jax.numpy as jnp
import numpy as np
from jax import lax
from jax.experimental import pallas as pl
from jax.experimental.pallas import tpu as pltpu

D_MODEL = 1024
HEAD_DIM = 64
N_HEADS = 16
N_KV_HEADS = 4
GROUP = 4
ATT_BLOCK = 128
ROT_DIM = 16
ROPE_THETA = 500000.0
RET_HEADS = 4
RET_QK_DIM = 256
RET_V_DIM = 512
RET_V_TOTAL = 2048
RET_IN_DIM = 6144
RET_THETA = 10000.0
D_FF = 2816
NORM_EPS = 1e-6

LANES = 128
FF_TILE = 256
N_FF_TILES = D_FF // FF_TILE
TOKEN_TILE = 1024
RET_PROJ_TILE = 512
PROJ_SUB_BLOCKS = 2
FFN_TOKEN_TILE = 1024
FFN_SUB_BLOCKS = 2
ATT_Q_TILE = 512
ATT_LOOKAHEAD = 3
RET_BLOCK = 256
RET_UNROLL = 8
VMEM_LIMIT = 56 * 1024 * 1024

LOG2_E = 1.4426950408889634
ATT_Q_SCALE = HEAD_DIM ** -0.5 * LOG2_E

F32 = jnp.float32
BF16 = jnp.bfloat16


def _rms(x, g):
    return x * lax.rsqrt(jnp.mean(x * x, axis=-1, keepdims=True) + NORM_EPS) * g


def _const_spec(shape):
    zeros = (0,) * len(shape)
    return pl.BlockSpec(shape, lambda *_: zeros, pipeline_mode=pl.Buffered(1))


def _params(sem):
    return pltpu.CompilerParams(dimension_semantics=sem, vmem_limit_bytes=VMEM_LIMIT)


def _ffn_hidden(x, rows, gpre_ref, wi_ref, act_ref):
    h = _rms(x, gpre_ref[...]).astype(BF16)
    for f in range(N_FF_TILES):
        gate = jnp.dot(h, wi_ref[:, f * FF_TILE:(f + 1) * FF_TILE], preferred_element_type=F32)
        up = jnp.dot(h, wi_ref[:, D_FF + f * FF_TILE:D_FF + (f + 1) * FF_TILE], preferred_element_type=F32)
        act_ref[rows, f * FF_TILE:(f + 1) * FF_TILE] = (gate * jax.nn.sigmoid(gate) * up).astype(BF16)


def _ffn_kernel(x_ref, gpre_ref, gpost_ref, wi_ref, wo_ref, o_ref, act_ref):
    sub = x_ref.shape[0] // FFN_SUB_BLOCKS
    for s in range(FFN_SUB_BLOCKS):
        rows = slice(s * sub, (s + 1) * sub)
        _ffn_hidden(x_ref[rows, :], rows, gpre_ref, wi_ref, act_ref)
    for s in range(FFN_SUB_BLOCKS):
        rows = slice(s * sub, (s + 1) * sub)
        y = jnp.dot(act_ref[rows, :], wo_ref[...], preferred_element_type=F32)
        o_ref[rows, :] = x_ref[rows, :] + 0.5 * _rms(y, gpost_ref[...])


def _slab_spec(stacked, layer, slot):
    return pl.BlockSpec((None, None) + stacked.shape[2:], lambda *_: (layer, slot, 0, 0),
                        pipeline_mode=pl.Buffered(1))


def _ffn(x, g_pre, g_post, wi, wo, layer, slot):
    n = x.shape[0]
    tm = FFN_TOKEN_TILE
    row = pl.BlockSpec((tm, D_MODEL), lambda i: (i, 0))
    return pl.pallas_call(
        _ffn_kernel,
        grid=(n // tm,),
        in_specs=[row, _const_spec((1, D_MODEL)), _const_spec((1, D_MODEL)),
                  _slab_spec(wi, layer, slot), _slab_spec(wo, layer, slot)],
        out_specs=row,
        out_shape=jax.ShapeDtypeStruct((n, D_MODEL), F32),
        scratch_shapes=[pltpu.VMEM((tm, D_FF), BF16)],
        compiler_params=_params(("parallel",)),
        name="ffn",
    )(x, g_pre, g_post, wi, wo)


def _attn_proj_kernel(x_ref, g_ref, w_ref, c_ref, s1_ref, s2_ref, q_ref, k_ref, v_ref):
    n_q = N_HEADS * HEAD_DIM // LANES
    n_k = N_KV_HEADS
    sub = x_ref.shape[0] // PROJ_SUB_BLOCKS
    for s in range(PROJ_SUB_BLOCKS):
        rows = slice(s * sub, (s + 1) * sub)
        h = _rms(x_ref[rows, :], g_ref[...]).astype(BF16)
        qkv = jnp.dot(h, w_ref[...], preferred_element_type=F32)
        c, s1, s2 = c_ref[rows, :], s1_ref[rows, :], s2_ref[rows, :]
        for j in range(n_q + n_k):
            xj = qkv[:, j * LANES:(j + 1) * LANES]
            rot = xj * c + pltpu.roll(xj, LANES - ROT_DIM // 2, 1) * s1 + pltpu.roll(xj, ROT_DIM // 2, 1) * s2
            if j < n_q:
                q_ref[rows, j * LANES:(j + 1) * LANES] = (rot * ATT_Q_SCALE).astype(BF16)
            else:
                jk = j - n_q
                k_ref[rows, jk * LANES:(jk + 1) * LANES] = rot.astype(BF16)
        v_ref[rows, :] = qkv[:, (n_q + n_k) * LANES:].astype(BF16)


def _attn_proj(x, g, w, tabs, seq_len):
    n = x.shape[0]
    tm = TOKEN_TILE
    per_seq = seq_len // tm
    row = pl.BlockSpec((tm, D_MODEL), lambda i: (i, 0))
    tab = pl.BlockSpec((tm, LANES), lambda i: (i % per_seq, 0))
    kv_w = N_KV_HEADS * LANES
    return pl.pallas_call(
        _attn_proj_kernel,
        grid=(n // tm,),
        in_specs=[row, _const_spec((1, D_MODEL)), _const_spec(w.shape), tab, tab, tab],
        out_specs=[row, pl.BlockSpec((tm, kv_w), lambda i: (i, 0)), pl.BlockSpec((tm, kv_w), lambda i: (i, 0))],
        out_shape=[jax.ShapeDtypeStruct((n, D_MODEL), BF16), jax.ShapeDtypeStruct((n, kv_w), BF16),
                   jax.ShapeDtypeStruct((n, kv_w), BF16)],
        compiler_params=_params(("parallel",)),
        name="attn_proj",
    )(x, g, w, *tabs)


def _attn_core_kernel(sink_ref, q_ref, kp_ref, km_ref, kn_ref, vp_ref, vm_ref, vn_ref, x_ref, g_ref, wo_ref,
                      o_ref, kext_ref, vt_ref, obuf_ref, bias_ref, *, n_blocks):
    tq = q_ref.shape[0]
    blocks = tq // ATT_BLOCK
    i = pl.program_id(1)
    blk = ATT_BLOCK
    kext_ref[0:blk] = kp_ref[...]
    kext_ref[blk:blk + tq] = km_ref[...]
    kext_ref[blk + tq:] = kn_ref[...]
    vt_ref[:, 0:blk] = vp_ref[...].T
    vt_ref[:, blk:blk + tq] = vm_ref[...].T
    vt_ref[:, blk + tq:] = vn_ref[...].T

    heads_per_col = LANES // HEAD_DIM
    width = heads_per_col * blk
    n_cols = N_HEADS // heads_per_col
    key = lax.broadcasted_iota(jnp.int32, (blk, width), 0)
    qlane = lax.broadcasted_iota(jnp.int32, (blk, width), 1)
    qpos = qlane & (blk - 1)
    neg_inf = jnp.float32(-jnp.inf)
    bias_ref[0] = jnp.where(key >= qpos, 0.0, neg_inf)
    bias_ref[1] = jnp.where(key <= qpos, 0.0, neg_inf)
    lane = lax.broadcasted_iota(jnp.int32, (blk, LANES), 1)
    lo = lane < HEAD_DIM
    top = lax.broadcasted_iota(jnp.int32, (LANES, blk), 0) < HEAD_DIM
    first_head = lax.broadcasted_iota(jnp.int32, (1, width), 1) < blk
    sink_rows = [jnp.where(first_head, sink_ref[heads_per_col * c], sink_ref[heads_per_col * c + 1])
                 for c in range(n_cols)]

    def scores(qb, c):
        r0 = qb * blk
        g = c * heads_per_col // GROUP
        kg = kext_ref[r0:r0 + 3 * blk, g * LANES:(g + 1) * LANES]
        qp = q_ref[r0:r0 + blk, c * LANES:(c + 1) * LANES]
        zero = jnp.zeros_like(qp)
        qs = jnp.concatenate([jnp.where(lo, qp, zero), jnp.where(lo, zero, qp)], axis=0)
        return lax.dot_general(kg, qs, (((1,), (1,)), ((), ())), preferred_element_type=F32)

    def finish(qb, c, st):
        gb = i * blocks + qb
        r0 = qb * blk
        g = c * heads_per_col // GROUP
        s_prev = st[:blk] + (bias_ref[0] + jnp.where(gb == 0, neg_inf, 0.0))
        s_mid = st[blk:2 * blk]
        s_next = st[2 * blk:] + (bias_ref[1] + jnp.where(gb == n_blocks - 1, neg_inf, 0.0))
        m = jnp.max(jnp.maximum(jnp.maximum(s_prev, s_mid), s_next), axis=0, keepdims=True)
        m = jnp.maximum(m, sink_rows[c])
        p_prev, p_mid, p_next = jnp.exp2(s_prev - m), jnp.exp2(s_mid - m), jnp.exp2(s_next - m)
        denom = jnp.sum(p_prev + p_mid + p_next, axis=0, keepdims=True) + jnp.exp2(sink_rows[c] - m)
        pt = jnp.concatenate([p_prev, p_mid, p_next], axis=0).astype(BF16)
        vt = vt_ref[g * LANES:(g + 1) * LANES, r0:r0 + 3 * blk]
        ot = jnp.dot(vt, pt, preferred_element_type=F32) * (1.0 / denom)
        obuf_ref[r0:r0 + blk, c * LANES:(c + 1) * LANES] = jnp.where(top, ot[:, :blk], ot[:, blk:]).T.astype(BF16)

    tiles = [(qb, c) for qb in range(blocks) for c in range(n_cols)]
    pending = [scores(*tile) for tile in tiles[:ATT_LOOKAHEAD]]
    for t, (qb, c) in enumerate(tiles):
        if t + ATT_LOOKAHEAD < len(tiles):
            pending.append(scores(*tiles[t + ATT_LOOKAHEAD]))
        finish(qb, c, pending.pop(0))

    y = jnp.dot(obuf_ref[...], wo_ref[...], preferred_element_type=F32)
    o_ref[...] = x_ref[...] + _rms(y, g_ref[...])


def _attn_core(sink, q, kd, vd, x, g, wo, batch, seq_len):
    tq = ATT_Q_TILE
    blocks = tq // ATT_BLOCK
    n_blocks = seq_len // ATT_BLOCK
    kv_w = N_KV_HEADS * LANES
    q3 = q.reshape(batch, seq_len, D_MODEL)
    k3 = kd.reshape(batch, seq_len, kv_w)
    v3 = vd.reshape(batch, seq_len, kv_w)
    x3 = x.reshape(batch, seq_len, D_MODEL)
    row = pl.BlockSpec((None, tq, D_MODEL), lambda b, i: (b, i, 0))
    kv_main = pl.BlockSpec((None, tq, kv_w), lambda b, i: (b, i, 0))
    kv_prev = pl.BlockSpec((None, ATT_BLOCK, kv_w), lambda b, i: (b, jnp.maximum(i * blocks - 1, 0), 0))
    kv_next = pl.BlockSpec((None, ATT_BLOCK, kv_w), lambda b, i: (b, jnp.minimum((i + 1) * blocks, n_blocks - 1), 0))
    out = pl.pallas_call(
        functools.partial(_attn_core_kernel, n_blocks=n_blocks),
        grid=(batch, seq_len // tq),
        in_specs=[pl.BlockSpec(memory_space=pltpu.SMEM), row, kv_prev, kv_main, kv_next, kv_prev, kv_main, kv_next,
                  row, _const_spec((1, D_MODEL)), _const_spec(wo.shape)],
        out_specs=row,
        out_shape=jax.ShapeDtypeStruct((batch, seq_len, D_MODEL), F32),
        scratch_shapes=[pltpu.VMEM((tq + 2 * ATT_BLOCK, kv_w), BF16), pltpu.VMEM((kv_w, tq + 2 * ATT_BLOCK), BF16),
                        pltpu.VMEM((tq, D_MODEL), BF16),
                        pltpu.VMEM((2, ATT_BLOCK, LANES // HEAD_DIM * ATT_BLOCK), F32)],
        compiler_params=_params(("parallel", "parallel")),
        name="attn_core",
    )(sink, q3, k3, k3, k3, v3, v3, v3, x3, g, wo)
    return out.reshape(batch * seq_len, D_MODEL)


def _ret_proj_kernel(x_ref, g_ref, w_ref, cos_ref, sin_ref, o_ref):
    half = RET_QK_DIM // 2
    n_chunks = RET_IN_DIM // D_MODEL
    sub = x_ref.shape[0] // PROJ_SUB_BLOCKS
    for s in range(PROJ_SUB_BLOCKS):
        rows = slice(s * sub, (s + 1) * sub)
        h = _rms(x_ref[rows, :], g_ref[...]).astype(BF16)
        cos, sin = cos_ref[rows, :], sin_ref[rows, :]
        for c in range(n_chunks):
            y = jnp.dot(h, w_ref[:, c * D_MODEL:(c + 1) * D_MODEL], preferred_element_type=F32)
            if c < 2:
                scale = 1.0 if c == 0 else RET_QK_DIM ** -0.5
                for hd in range(RET_HEADS):
                    x1 = y[:, hd * RET_QK_DIM:hd * RET_QK_DIM + half]
                    x2 = y[:, hd * RET_QK_DIM + half:(hd + 1) * RET_QK_DIM]
                    base = c * D_MODEL + hd * RET_QK_DIM
                    o_ref[rows, base:base + half] = ((x1 * cos - x2 * sin) * scale).astype(BF16)
                    o_ref[rows, base + half:base + 2 * half] = ((x2 * cos + x1 * sin) * scale).astype(BF16)
            elif c < 4:
                o_ref[rows, c * D_MODEL:(c + 1) * D_MODEL] = y.astype(BF16)
            else:
                o_ref[rows, c * D_MODEL:(c + 1) * D_MODEL] = (y * jax.nn.sigmoid(y)).astype(BF16)


def _ret_proj(x, g, w, tabs, seq_len):
    n = x.shape[0]
    tm = RET_PROJ_TILE
    per_seq = seq_len // tm
    row = pl.BlockSpec((tm, D_MODEL), lambda i: (i, 0))
    tab = pl.BlockSpec((tm, LANES), lambda i: (i % per_seq, 0))
    return pl.pallas_call(
        _ret_proj_kernel,
        grid=(n // tm,),
        in_specs=[row, _const_spec((1, D_MODEL)), _const_spec(w.shape), tab, tab],
        out_specs=pl.BlockSpec((tm, RET_IN_DIM), lambda i: (i, 0)),
        out_shape=jax.ShapeDtypeStruct((n, RET_IN_DIM), BF16),
        compiler_params=_params(("parallel",)),
        name="ret_proj",
    )(x, g, w, *tabs)


def _ret_core_kernel(lgf_ref, lgb_ref, q_ref, k_ref, v_ref, o_ref, acc_ref, st_ref, dmat_ref):
    blk = RET_BLOCK
    seq_len = q_ref.shape[0]
    n_chunks = seq_len // blk
    hd = pl.program_id(1)
    lgf = lgf_ref[hd]
    lgb = lgb_ref[hd]
    ii = lax.broadcasted_iota(jnp.int32, (blk, blk), 0)
    jj = lax.broadcasted_iota(jnp.int32, (blk, blk), 1)
    diff = (ii - jj).astype(F32)
    dmat_ref[...] = jnp.where(diff >= 0, jnp.exp(lgf * jnp.maximum(diff, 0.0)),
                              jnp.exp(lgb * jnp.maximum(-diff, 0.0)))
    idx = lax.broadcasted_iota(jnp.int32, (blk, 1), 0).astype(F32)
    qdec_f = jnp.exp(lgf * (idx + 1.0))
    kdec_f = jnp.exp(lgf * (blk - 1.0 - idx))
    cdec_f = jnp.exp(lgf * blk)
    qdec_b = jnp.exp(lgb * (blk - idx))
    kdec_b = jnp.exp(lgb * idx)
    cdec_b = jnp.exp(lgb * blk)

    def state_delta(kc, vc, kdec):
        kd = (kc.astype(F32) * kdec).astype(BF16)
        return lax.dot_general(kd, vc, (((0,), (0,)), ((), ())), preferred_element_type=F32)

    st_ref[...] = jnp.zeros_like(st_ref)

    def fwd(c, carry):
        rows = pl.ds(pl.multiple_of(c * blk, blk), blk)
        qc, kc, vc = q_ref[rows, :], k_ref[rows, :], v_ref[rows, :]
        s = lax.dot_general(qc, kc, (((1,), (1,)), ((), ())), preferred_element_type=F32)
        delta = state_delta(kc, vc, kdec_f)
        cross = jnp.dot(qc, st_ref[...].astype(BF16), preferred_element_type=F32) * qdec_f
        intra = jnp.dot((s * dmat_ref[...]).astype(BF16), vc, preferred_element_type=F32)
        acc_ref[rows, :] = intra + cross
        st_ref[...] = cdec_f * st_ref[...] + delta
        return carry

    lax.fori_loop(0, n_chunks, fwd, 0, unroll=RET_UNROLL)

    st_ref[...] = jnp.zeros_like(st_ref)

    def bwd(t, carry):
        c = n_chunks - 1 - t
        rows = pl.ds(pl.multiple_of(c * blk, blk), blk)
        qc, kc, vc = q_ref[rows, :], k_ref[rows, :], v_ref[rows, :]
        delta = state_delta(kc, vc, kdec_b)
        cross = jnp.dot(qc, st_ref[...].astype(BF16), preferred_element_type=F32) * qdec_b
        o_ref[rows, :] = (acc_ref[rows, :] + cross).astype(BF16)
        st_ref[...] = cdec_b * st_ref[...] + delta
        return carry

    lax.fori_loop(0, n_chunks, bwd, 0, unroll=RET_UNROLL)


def _ret_core(lgf, lgb, proj, batch, seq_len):
    p3 = proj.reshape(batch, seq_len, RET_IN_DIM)
    qk_blocks = D_MODEL // RET_QK_DIM
    v_off = 2 * D_MODEL // RET_V_DIM
    smem = pl.BlockSpec(memory_space=pltpu.SMEM)
    out = pl.pallas_call(
        _ret_core_kernel,
        grid=(batch, RET_HEADS),
        in_specs=[smem, smem,
                  pl.BlockSpec((None, seq_len, RET_QK_DIM), lambda b, h: (b, 0, h)),
                  pl.BlockSpec((None, seq_len, RET_QK_DIM), lambda b, h: (b, 0, qk_blocks + h)),
                  pl.BlockSpec((None, seq_len, RET_V_DIM), lambda b, h: (b, 0, v_off + h))],
        out_specs=pl.BlockSpec((None, seq_len, RET_V_DIM), lambda b, h: (b, 0, h)),
        out_shape=jax.ShapeDtypeStruct((batch, seq_len, RET_V_TOTAL), BF16),
        scratch_shapes=[pltpu.VMEM((seq_len, RET_V_DIM), F32), pltpu.VMEM((RET_QK_DIM, RET_V_DIM), F32),
                        pltpu.VMEM((RET_BLOCK, RET_BLOCK), F32)],
        compiler_params=_params(("parallel", "parallel")),
        name="ret_core",
    )(lgf, lgb, p3, p3, p3)
    return out.reshape(batch * seq_len, RET_V_TOTAL)


def _ret_out_kernel(y_ref, gate_ref, x_ref, g_ref, w_ref, o_ref, z_ref):
    sub = x_ref.shape[0] // PROJ_SUB_BLOCKS
    for s in range(PROJ_SUB_BLOCKS):
        rows = slice(s * sub, (s + 1) * sub)
        for hd in range(RET_HEADS):
            cols = slice(hd * RET_V_DIM, (hd + 1) * RET_V_DIM)
            y = y_ref[rows, cols].astype(F32)
            yn = y * lax.rsqrt(jnp.mean(y * y, axis=-1, keepdims=True) + NORM_EPS)
            z_ref[rows, cols] = (gate_ref[rows, cols].astype(F32) * yn).astype(BF16)
        out = jnp.dot(z_ref[rows, :], w_ref[...], preferred_element_type=F32)
        o_ref[rows, :] = x_ref[rows, :] + _rms(out, g_ref[...])


def _ret_out(y, proj, x, g, w):
    n = x.shape[0]
    tm = TOKEN_TILE
    row = pl.BlockSpec((tm, D_MODEL), lambda i: (i, 0))
    wide = pl.BlockSpec((tm, RET_V_TOTAL), lambda i: (i, 0))
    gate_cols = pl.BlockSpec((tm, RET_V_TOTAL), lambda i: (i, RET_IN_DIM // RET_V_TOTAL - 1))
    return pl.pallas_call(
        _ret_out_kernel,
        grid=(n // tm,),
        in_specs=[wide, gate_cols, row, _const_spec((1, D_MODEL)), _const_spec(w.shape)],
        out_specs=row,
        out_shape=jax.ShapeDtypeStruct((n, D_MODEL), F32),
        scratch_shapes=[pltpu.VMEM((tm, RET_V_TOTAL), BF16)],
        compiler_params=_params(("parallel",)),
        name="ret_out",
    )(y, proj, x, g, w)


def _angles(seq_len, half, theta):
    inv_freq = theta ** (-np.arange(half, dtype=np.float64) / half)
    return np.arange(seq_len, dtype=np.float64)[:, None] * inv_freq[None, :]


def _attn_tables(seq_len):
    half = ROT_DIM // 2
    ang = _angles(seq_len, half, ROPE_THETA)
    cos, sin = np.cos(ang), np.sin(ang)
    pad = HEAD_DIM - ROT_DIM
    ones = np.ones((seq_len, pad))
    zeros = np.zeros((seq_len, pad))
    zh = np.zeros((seq_len, half))
    c = np.concatenate([cos, cos, ones], axis=1)
    s1 = np.concatenate([-sin, zh, zeros], axis=1)
    s2 = np.concatenate([zh, sin, zeros], axis=1)
    reps = LANES // HEAD_DIM
    return tuple(jnp.asarray(np.tile(t, (1, reps)), F32) for t in (c, s1, s2))


def _ret_tables(seq_len):
    ang = _angles(seq_len, RET_QK_DIM // 2, RET_THETA)
    return jnp.asarray(np.cos(ang), F32), jnp.asarray(np.sin(ang), F32)


def _attn_weights(w_qkv):
    q_dim = N_HEADS * HEAD_DIM
    kv_dim = N_KV_HEADS * HEAD_DIM
    wq = w_qkv[:, :q_dim]
    wk = w_qkv[:, q_dim:q_dim + kv_dim].reshape(D_MODEL, N_KV_HEADS, 1, HEAD_DIM)
    wv = w_qkv[:, q_dim + kv_dim:].reshape(D_MODEL, N_KV_HEADS, 1, HEAD_DIM)
    reps = LANES // HEAD_DIM
    wk = jnp.broadcast_to(wk, (D_MODEL, N_KV_HEADS, reps, HEAD_DIM)).reshape(D_MODEL, N_KV_HEADS * LANES)
    wv = jnp.broadcast_to(wv, (D_MODEL, N_KV_HEADS, reps, HEAD_DIM)).reshape(D_MODEL, N_KV_HEADS * LANES)
    return jnp.concatenate([wq, wk, wv], axis=1).astype(BF16)


def _trunk(x, gains, ffn_w, attn_w, attn_wo, sink, ret_w, ret_wo, lgf, lgb, attn_tabs, ret_tabs):
    batch, seq_len, _ = x.shape
    x = x.reshape(batch * seq_len, D_MODEL)
    for layer in range(2):
        g = gains[layer]
        x = _ffn(x, g[0], g[1], *ffn_w, layer, 0)
        if layer == 0:
            q, kd, vd = _attn_proj(x, g[2], attn_w, attn_tabs, seq_len)
            x = _attn_core(sink, q, kd, vd, x, g[3], attn_wo, batch, seq_len)
        else:
            proj = _ret_proj(x, g[2], ret_w, ret_tabs, seq_len)
            y = _ret_core(lgf, lgb, proj, batch, seq_len)
            x = _ret_out(y, proj, x, g[3], ret_wo)
        x = _ffn(x, g[4], g[5], *ffn_w, layer, 1)
    return x.reshape(batch, seq_len, D_MODEL)


def kernel(x_prompt, x_sample, norm_gains, ffn_w_in, ffn_w_out, attn_w_qkv, attn_w_o, attn_sink, ret_w_in, ret_w_o, ret_decay_fwd, ret_decay_bwd):
    gains = norm_gains.astype(F32).reshape(2, 6, 1, D_MODEL)
    ffn_w = (ffn_w_in.astype(BF16), ffn_w_out.astype(BF16))
    attn_w = _attn_weights(attn_w_qkv[0])
    attn_wo = attn_w_o[0].astype(BF16)
    sink = attn_sink[0].astype(F32) * LOG2_E
    ret_w = ret_w_in[0].astype(BF16)
    ret_wo = ret_w_o[0].astype(BF16)
    lgf = jax.nn.log_sigmoid(ret_decay_fwd[0].astype(F32))
    lgb = jax.nn.log_sigmoid(ret_decay_bwd[0].astype(F32))
    outs = []
    for x in (x_prompt, x_sample):
        seq_len = x.shape[1]
        outs.append(_trunk(x, gains, ffn_w, attn_w, attn_wo, sink, ret_w, ret_wo, lgf, lgb,
                           _attn_tables(seq_len), _ret_tables(seq_len)))
    return tuple(outs)
```

```python
import functools

import jax
import jax.numpy as jnp
import numpy as np
from jax import lax
from jax.experimental import pallas as pl
from jax.experimental.pallas import tpu as pltpu

D_MODEL = 1024
HEAD_DIM = 64
N_HEADS = 16
N_KV_HEADS = 4
GROUP = 4
ATT_BLOCK = 128
ROT_DIM = 16
ROPE_THETA = 500000.0
RET_HEADS = 4
RET_QK_DIM = 256
RET_V_DIM = 512
RET_V_TOTAL = 2048
RET_IN_DIM = 6144
RET_THETA = 10000.0
D_FF = 2816
NORM_EPS = 1e-6

LANES = 128
FF_TILE = 256
N_FF_TILES = D_FF // FF_TILE
TOKEN_TILE = 1024
RET_PROJ_TILE = 512
PROJ_SUB_BLOCKS = 2
FFN_TOKEN_TILE = 1024
FFN_SUB_BLOCKS = 4
ATT_Q_TILE = 1024
ATT_LOOKAHEAD = 3
RET_BLOCK = 256
RET_UNROLL = 16
VMEM_LIMIT = 56 * 1024 * 1024

LOG2_E = 1.4426950408889634
ATT_Q_SCALE = HEAD_DIM ** -0.5 * LOG2_E

F32 = jnp.float32
BF16 = jnp.bfloat16


def _rms(x, g):
    return x * lax.rsqrt(jnp.mean(x * x, axis=-1, keepdims=True) + NORM_EPS) * g


def _const_spec(shape):
    zeros = (0,) * len(shape)
    return pl.BlockSpec(shape, lambda *_: zeros, pipeline_mode=pl.Buffered(1))


def _params(sem):
    return pltpu.CompilerParams(dimension_semantics=sem, vmem_limit_bytes=VMEM_LIMIT)


def _ffn_hidden(x, rows, gpre_ref, wi_ref, act_ref):
    h = _rms(x, gpre_ref[...]).astype(BF16)
    for f in range(N_FF_TILES):
        gate = jnp.dot(h, wi_ref[:, f * FF_TILE:(f + 1) * FF_TILE], preferred_element_type=F32)
        up = jnp.dot(h, wi_ref[:, D_FF + f * FF_TILE:D_FF + (f + 1) * FF_TILE], preferred_element_type=F32)
        act_ref[rows, f * FF_TILE:(f + 1) * FF_TILE] = (gate * jax.nn.sigmoid(gate) * up).astype(BF16)


def _ffn_kernel(x_ref, gpre_ref, gpost_ref, wi_ref, wo_ref, o_ref, act_ref):
    sub = x_ref.shape[0] // FFN_SUB_BLOCKS
    for s in range(FFN_SUB_BLOCKS):
        rows = slice(s * sub, (s + 1) * sub)
        _ffn_hidden(x_ref[rows, :], rows, gpre_ref, wi_ref, act_ref)
    for s in range(FFN_SUB_BLOCKS):
        rows = slice(s * sub, (s + 1) * sub)
        y = jnp.dot(act_ref[rows, :], wo_ref[...], preferred_element_type=F32)
        o_ref[rows, :] = x_ref[rows, :] + 0.5 * _rms(y, gpost_ref[...])


def _slab_spec(stacked, layer, slot):
    return pl.BlockSpec((None, None) + stacked.shape[2:], lambda *_: (layer, slot, 0, 0),
                        pipeline_mode=pl.Buffered(1))


def _ffn(x, g_pre, g_post, wi, wo, layer, slot):
    n = x.shape[0]
    tm = FFN_TOKEN_TILE
    row = pl.BlockSpec((tm, D_MODEL), lambda i: (i, 0))
    return pl.pallas_call(
        _ffn_kernel,
        grid=(n // tm,),
        in_specs=[row, _const_spec((1, D_MODEL)), _const_spec((1, D_MODEL)),
                  _slab_spec(wi, layer, slot), _slab_spec(wo, layer, slot)],
        out_specs=row,
        out_shape=jax.ShapeDtypeStruct((n, D_MODEL), F32),
        scratch_shapes=[pltpu.VMEM((tm, D_FF), BF16)],
        compiler_params=_params(("parallel",)),
        name="ffn",
    )(x, g_pre, g_post, wi, wo)


def _attn_proj_kernel(x_ref, g_ref, w_ref, c_ref, s1_ref, s2_ref, q_ref, k_ref, v_ref):
    n_q = N_HEADS * HEAD_DIM // LANES
    n_k = N_KV_HEADS
    sub = x_ref.shape[0] // PROJ_SUB_BLOCKS
    for s in range(PROJ_SUB_BLOCKS):
        rows = slice(s * sub, (s + 1) * sub)
        h = _rms(x_ref[rows, :], g_ref[...]).astype(BF16)
        qkv = jnp.dot(h, w_ref[...], preferred_element_type=F32)
        c, s1, s2 = c_ref[rows, :], s1_ref[rows, :], s2_ref[rows, :]
        for j in range(n_q + n_k):
            xj = qkv[:, j * LANES:(j + 1) * LANES]
            rot = xj * c + pltpu.roll(xj, LANES - ROT_DIM // 2, 1) * s1 + pltpu.roll(xj, ROT_DIM // 2, 1) * s2
            if j < n_q:
                q_ref[rows, j * LANES:(j + 1) * LANES] = (rot * ATT_Q_SCALE).astype(BF16)
            else:
                jk = j - n_q
                k_ref[rows, jk * LANES:(jk + 1) * LANES] = rot.astype(BF16)
        v_ref[rows, :] = qkv[:, (n_q + n_k) * LANES:].astype(BF16)


def _attn_proj(x, g, w, tabs, seq_len):
    n = x.shape[0]
    tm = TOKEN_TILE
    per_seq = seq_len // tm
    row = pl.BlockSpec((tm, D_MODEL), lambda i: (i, 0))
    tab = pl.BlockSpec((tm, LANES), lambda i: (i % per_seq, 0))
    kv_w = N_KV_HEADS * LANES
    return pl.pallas_call(
        _attn_proj_kernel,
        grid=(n // tm,),
        in_specs=[row, _const_spec((1, D_MODEL)), _const_spec(w.shape), tab, tab, tab],
        out_specs=[row, pl.BlockSpec((tm, kv_w), lambda i: (i, 0)), pl.BlockSpec((tm, kv_w), lambda i: (i, 0))],
        out_shape=[jax.ShapeDtypeStruct((n, D_MODEL), BF16), jax.ShapeDtypeStruct((n, kv_w), BF16),
                   jax.ShapeDtypeStruct((n, kv_w), BF16)],
        compiler_params=_params(("parallel",)),
        name="attn_proj",
    )(x, g, w, *tabs)


def _attn_core_kernel(sink_ref, q_ref, kp_ref, km_ref, kn_ref, vp_ref, vm_ref, vn_ref, x_ref, g_ref, wo_ref,
                      o_ref, kext_ref, vt_ref, obuf_ref, bias_ref, *, n_blocks):
    tq = q_ref.shape[0]
    blocks = tq // ATT_BLOCK
    i = pl.program_id(1)
    blk = ATT_BLOCK
    kext_ref[0:blk] = kp_ref[...]
    kext_ref[blk:blk + tq] = km_ref[...]
    kext_ref[blk + tq:] = kn_ref[...]
    vt_ref[:, 0:blk] = vp_ref[...].T
    vt_ref[:, blk:blk + tq] = vm_ref[...].T
    vt_ref[:, blk + tq:] = vn_ref[...].T

    heads_per_col = LANES // HEAD_DIM
    width = heads_per_col * blk
    n_cols = N_HEADS // heads_per_col
    key = lax.broadcasted_iota(jnp.int32, (blk, width), 0)
    qlane = lax.broadcasted_iota(jnp.int32, (blk, width), 1)
    qpos = qlane & (blk - 1)
    neg_inf = jnp.float32(-jnp.inf)
    bias_ref[0] = jnp.where(key >= qpos, 0.0, neg_inf)
    bias_ref[1] = jnp.where(key <= qpos, 0.0, neg_inf)
    lane = lax.broadcasted_iota(jnp.int32, (blk, LANES), 1)
    lo = lane < HEAD_DIM
    top = lax.broadcasted_iota(jnp.int32, (LANES, blk), 0) < HEAD_DIM
    first_head = lax.broadcasted_iota(jnp.int32, (1, width), 1) < blk
    sink_rows = [jnp.where(first_head, sink_ref[heads_per_col * c], sink_ref[heads_per_col * c + 1])
                 for c in range(n_cols)]

    def scores(qb, c):
        r0 = qb * blk
        g = c * heads_per_col // GROUP
        kg = kext_ref[r0:r0 + 3 * blk, g * LANES:(g + 1) * LANES]
        qp = q_ref[r0:r0 + blk, c * LANES:(c + 1) * LANES]
        zero = jnp.zeros_like(qp)
        qs = jnp.concatenate([jnp.where(lo, qp, zero), jnp.where(lo, zero, qp)], axis=0)
        return lax.dot_general(kg, qs, (((1,), (1,)), ((), ())), preferred_element_type=F32)

    def finish(qb, c, st):
        gb = i * blocks + qb
        r0 = qb * blk
        g = c * heads_per_col // GROUP
        s_prev = st[:blk] + (bias_ref[0] + jnp.where(gb == 0, neg_inf, 0.0))
        s_mid = st[blk:2 * blk]
        s_next = st[2 * blk:] + (bias_ref[1] + jnp.where(gb == n_blocks - 1, neg_inf, 0.0))
        m = jnp.max(jnp.maximum(jnp.maximum(s_prev, s_mid), s_next), axis=0, keepdims=True)
        m = jnp.maximum(m, sink_rows[c])
        p_prev, p_mid, p_next = jnp.exp2(s_prev - m), jnp.exp2(s_mid - m), jnp.exp2(s_next - m)
        denom = jnp.sum(p_prev + p_mid + p_next, axis=0, keepdims=True) + jnp.exp2(sink_rows[c] - m)
        pt = jnp.concatenate([p_prev, p_mid, p_next], axis=0).astype(BF16)
        vt = vt_ref[g * LANES:(g + 1) * LANES, r0:r0 + 3 * blk]
        ot = jnp.dot(vt, pt, preferred_element_type=F32) * (1.0 / denom)
        obuf_ref[r0:r0 + blk, c * LANES:(c + 1) * LANES] = jnp.where(top, ot[:, :blk], ot[:, blk:]).T.astype(BF16)

    tiles = [(qb, c) for qb in range(blocks) for c in range(n_cols)]
    pending = [scores(*tile) for tile in tiles[:ATT_LOOKAHEAD]]
    for t, (qb, c) in enumerate(tiles):
        if t + ATT_LOOKAHEAD < len(tiles):
            pending.append(scores(*tiles[t + ATT_LOOKAHEAD]))
        finish(qb, c, pending.pop(0))

    y = jnp.dot(obuf_ref[...], wo_ref[...], preferred_element_type=F32)
    o_ref[...] = x_ref[...] + _rms(y, g_ref[...])


def _attn_core(sink, q, kd, vd, x, g, wo, batch, seq_len):
    tq = ATT_Q_TILE
    blocks = tq // ATT_BLOCK
    n_blocks = seq_len // ATT_BLOCK
    kv_w = N_KV_HEADS * LANES
    q3 = q.reshape(batch, seq_len, D_MODEL)
    k3 = kd.reshape(batch, seq_len, kv_w)
    v3 = vd.reshape(batch, seq_len, kv_w)
    x3 = x.reshape(batch, seq_len, D_MODEL)
    row = pl.BlockSpec((None, tq, D_MODEL), lambda b, i: (b, i, 0))
    kv_main = pl.BlockSpec((None, tq, kv_w), lambda b, i: (b, i, 0))
    kv_prev = pl.BlockSpec((None, ATT_BLOCK, kv_w), lambda b, i: (b, jnp.maximum(i * blocks - 1, 0), 0))
    kv_next = pl.BlockSpec((None, ATT_BLOCK, kv_w), lambda b, i: (b, jnp.minimum((i + 1) * blocks, n_blocks - 1), 0))
    out = pl.pallas_call(
        functools.partial(_attn_core_kernel, n_blocks=n_blocks),
        grid=(batch, seq_len // tq),
        in_specs=[pl.BlockSpec(memory_space=pltpu.SMEM), row, kv_prev, kv_main, kv_next, kv_prev, kv_main, kv_next,
                  row, _const_spec((1, D_MODEL)), _const_spec(wo.shape)],
        out_specs=row,
        out_shape=jax.ShapeDtypeStruct((batch, seq_len, D_MODEL), F32),
        scratch_shapes=[pltpu.VMEM((tq + 2 * ATT_BLOCK, kv_w), BF16), pltpu.VMEM((kv_w, tq + 2 * ATT_BLOCK), BF16),
                        pltpu.VMEM((tq, D_MODEL), BF16),
                        pltpu.VMEM((2, ATT_BLOCK, LANES // HEAD_DIM * ATT_BLOCK), F32)],
        compiler_params=_params(("parallel", "parallel")),
        name="attn_core",
    )(sink, q3, k3, k3, k3, v3, v3, v3, x3, g, wo)
    return out.reshape(batch * seq_len, D_MODEL)


def _ret_proj_kernel(x_ref, g_ref, w_ref, cos_ref, sin_ref, o_ref):
    half = RET_QK_DIM // 2
    n_chunks = RET_IN_DIM // D_MODEL
    sub = x_ref.shape[0] // PROJ_SUB_BLOCKS
    for s in range(PROJ_SUB_BLOCKS):
        rows = slice(s * sub, (s + 1) * sub)
        h = _rms(x_ref[rows, :], g_ref[...]).astype(BF16)
        cos, sin = cos_ref[rows, :], sin_ref[rows, :]
        for c in range(n_chunks):
            y = jnp.dot(h, w_ref[:, c * D_MODEL:(c + 1) * D_MODEL], preferred_element_type=F32)
            if c < 2:
                scale = 1.0 if c == 0 else RET_QK_DIM ** -0.5
                for hd in range(RET_HEADS):
                    x1 = y[:, hd * RET_QK_DIM:hd * RET_QK_DIM + half]
                    x2 = y[:, hd * RET_QK_DIM + half:(hd + 1) * RET_QK_DIM]
                    base = c * D_MODEL + hd * RET_QK_DIM
                    o_ref[rows, base:base + half] = ((x1 * cos - x2 * sin) * scale).astype(BF16)
                    o_ref[rows, base + half:base + 2 * half] = ((x2 * cos + x1 * sin) * scale).astype(BF16)
            elif c < 4:
                o_ref[rows, c * D_MODEL:(c + 1) * D_MODEL] = y.astype(BF16)
            else:
                o_ref[rows, c * D_MODEL:(c + 1) * D_MODEL] = (y * jax.nn.sigmoid(y)).astype(BF16)


def _ret_proj(x, g, w, tabs, seq_len):
    n = x.shape[0]
    tm = RET_PROJ_TILE
    per_seq = seq_len // tm
    row = pl.BlockSpec((tm, D_MODEL), lambda i: (i, 0))
    tab = pl.BlockSpec((tm, LANES), lambda i: (i % per_seq, 0))
    return pl.pallas_call(
        _ret_proj_kernel,
        grid=(n // tm,),
        in_specs=[row, _const_spec((1, D_MODEL)), _const_spec(w.shape), tab, tab],
        out_specs=pl.BlockSpec((tm, RET_IN_DIM), lambda i: (i, 0)),
        out_shape=jax.ShapeDtypeStruct((n, RET_IN_DIM), BF16),
        compiler_params=_params(("parallel",)),
        name="ret_proj",
    )(x, g, w, *tabs)


def _ret_core_kernel(lgf_ref, lgb_ref, q_ref, k_ref, v_ref, o_ref, acc_ref, st_ref, dmat_ref):
    blk = RET_BLOCK
    seq_len = q_ref.shape[0]
    n_chunks = seq_len // blk
    hd = pl.program_id(1)
    lgf = lgf_ref[hd]
    lgb = lgb_ref[hd]
    ii = lax.broadcasted_iota(jnp.int32, (blk, blk), 0)
    jj = lax.broadcasted_iota(jnp.int32, (blk, blk), 1)
    diff = (ii - jj).astype(F32)
    dmat_ref[...] = jnp.where(diff >= 0, jnp.exp(lgf * jnp.maximum(diff, 0.0)),
                              jnp.exp(lgb * jnp.maximum(-diff, 0.0)))
    idx = lax.broadcasted_iota(jnp.int32, (blk, 1), 0).astype(F32)
    qdec_f = jnp.exp(lgf * (idx + 1.0))
    kdec_f = jnp.exp(lgf * (blk - 1.0 - idx))
    cdec_f = jnp.exp(lgf * blk)
    qdec_b = jnp.exp(lgb * (blk - idx))
    kdec_b = jnp.exp(lgb * idx)
    cdec_b = jnp.exp(lgb * blk)

    def state_delta(kc, vc, kdec):
        kd = (kc.astype(F32) * kdec).astype(BF16)
        return lax.dot_general(kd, vc, (((0,), (0,)), ((), ())), preferred_element_type=F32)

    st_ref[...] = jnp.zeros_like(st_ref)

    def fwd(c, carry):
        rows = pl.ds(pl.multiple_of(c * blk, blk), blk)
        qc, kc, vc = q_ref[rows, :], k_ref[rows, :], v_ref[rows, :]
        s = lax.dot_general(qc, kc, (((1,), (1,)), ((), ())), preferred_element_type=F32)
        delta = state_delta(kc, vc, kdec_f)
        cross = jnp.dot(qc, st_ref[...].astype(BF16), preferred_element_type=F32) * qdec_f
        intra = jnp.dot((s * dmat_ref[...]).astype(BF16), vc, preferred_element_type=F32)
        acc_ref[rows, :] = intra + cross
        st_ref[...] = cdec_f * st_ref[...] + delta
        return carry

    lax.fori_loop(0, n_chunks, fwd, 0, unroll=min(RET_UNROLL, n_chunks))

    st_ref[...] = jnp.zeros_like(st_ref)

    def bwd(t, carry):
        c = n_chunks - 1 - t
        rows = pl.ds(pl.multiple_of(c * blk, blk), blk)
        qc, kc, vc = q_ref[rows, :], k_ref[rows, :], v_ref[rows, :]
        delta = state_delta(kc, vc, kdec_b)
        cross = jnp.dot(qc, st_ref[...].astype(BF16), preferred_element_type=F32) * qdec_b
        o_ref[rows, :] = (acc_ref[rows, :] + cross).astype(BF16)
        st_ref[...] = cdec_b * st_ref[...] + delta
        return carry

    lax.fori_loop(0, n_chunks, bwd, 0, unroll=min(RET_UNROLL, n_chunks))


def _ret_core(lgf, lgb, proj, batch, seq_len):
    p3 = proj.reshape(batch, seq_len, RET_IN_DIM)
    qk_blocks = D_MODEL // RET_QK_DIM
    v_off = 2 * D_MODEL // RET_V_DIM
    smem = pl.BlockSpec(memory_space=pltpu.SMEM)
    out = pl.pallas_call(
        _ret_core_kernel,
        grid=(batch, RET_HEADS),
        in_specs=[smem, smem,
                  pl.BlockSpec((None, seq_len, RET_QK_DIM), lambda b, h: (b, 0, h)),
                  pl.BlockSpec((None, seq_len, RET_QK_DIM), lambda b, h: (b, 0, qk_blocks + h)),
                  pl.BlockSpec((None, seq_len, RET_V_DIM), lambda b, h: (b, 0, v_off + h))],
        out_specs=pl.BlockSpec((None, seq_len, RET_V_DIM), lambda b, h: (b, 0, h)),
        out_shape=jax.ShapeDtypeStruct((batch, seq_len, RET_V_TOTAL), BF16),
        scratch_shapes=[pltpu.VMEM((seq_len, RET_V_DIM), F32), pltpu.VMEM((RET_QK_DIM, RET_V_DIM), F32),
                        pltpu.VMEM((RET_BLOCK, RET_BLOCK), F32)],
        compiler_params=_params(("parallel", "parallel")),
        name="ret_core",
    )(lgf, lgb, p3, p3, p3)
    return out.reshape(batch * seq_len, RET_V_TOTAL)


def _ret_out_kernel(y_ref, gate_ref, x_ref, g_ref, w_ref, o_ref, z_ref):
    sub = x_ref.shape[0] // PROJ_SUB_BLOCKS
    for s in range(PROJ_SUB_BLOCKS):
        rows = slice(s * sub, (s + 1) * sub)
        for hd in range(RET_HEADS):
            cols = slice(hd * RET_V_DIM, (hd + 1) * RET_V_DIM)
            y = y_ref[rows, cols].astype(F32)
            yn = y * lax.rsqrt(jnp.mean(y * y, axis=-1, keepdims=True) + NORM_EPS)
            z_ref[rows, cols] = (gate_ref[rows, cols].astype(F32) * yn).astype(BF16)
        out = jnp.dot(z_ref[rows, :], w_ref[...], preferred_element_type=F32)
        o_ref[rows, :] = x_ref[rows, :] + _rms(out, g_ref[...])


def _ret_out(y, proj, x, g, w):
    n = x.shape[0]
    tm = TOKEN_TILE
    row = pl.BlockSpec((tm, D_MODEL), lambda i: (i, 0))
    wide = pl.BlockSpec((tm, RET_V_TOTAL), lambda i: (i, 0))
    gate_cols = pl.BlockSpec((tm, RET_V_TOTAL), lambda i: (i, RET_IN_DIM // RET_V_TOTAL - 1))
    return pl.pallas_call(
        _ret_out_kernel,
        grid=(n // tm,),
        in_specs=[wide, gate_cols, row, _const_spec((1, D_MODEL)), _const_spec(w.shape)],
        out_specs=row,
        out_shape=jax.ShapeDtypeStruct((n, D_MODEL), F32),
        scratch_shapes=[pltpu.VMEM((tm, RET_V_TOTAL), BF16)],
        compiler_params=_params(("parallel",)),
        name="ret_out",
    )(y, proj, x, g, w)


def _angles(seq_len, half, theta):
    inv_freq = theta ** (-np.arange(half, dtype=np.float64) / half)
    return np.arange(seq_len, dtype=np.float64)[:, None] * inv_freq[None, :]


def _attn_tables(seq_len):
    half = ROT_DIM // 2
    ang = _angles(seq_len, half, ROPE_THETA)
    cos, sin = np.cos(ang), np.sin(ang)
    pad = HEAD_DIM - ROT_DIM
    ones = np.ones((seq_len, pad))
    zeros = np.zeros((seq_len, pad))
    zh = np.zeros((seq_len, half))
    c = np.concatenate([cos, cos, ones], axis=1)
    s1 = np.concatenate([-sin, zh, zeros], axis=1)
    s2 = np.concatenate([zh, sin, zeros], axis=1)
    reps = LANES // HEAD_DIM
    return tuple(jnp.asarray(np.tile(t, (1, reps)), F32) for t in (c, s1, s2))


def _ret_tables(seq_len):
    ang = _angles(seq_len, RET_QK_DIM // 2, RET_THETA)
    return jnp.asarray(np.cos(ang), F32), jnp.asarray(np.sin(ang), F32)


def _attn_weights(w_qkv):
    q_dim = N_HEADS * HEAD_DIM
    kv_dim = N_KV_HEADS * HEAD_DIM
    wq = w_qkv[:, :q_dim]
    wk = w_qkv[:, q_dim:q_dim + kv_dim].reshape(D_MODEL, N_KV_HEADS, 1, HEAD_DIM)
    wv = w_qkv[:, q_dim + kv_dim:].reshape(D_MODEL, N_KV_HEADS, 1, HEAD_DIM)
    reps = LANES // HEAD_DIM
    wk = jnp.broadcast_to(wk, (D_MODEL, N_KV_HEADS, reps, HEAD_DIM)).reshape(D_MODEL, N_KV_HEADS * LANES)
    wv = jnp.broadcast_to(wv, (D_MODEL, N_KV_HEADS, reps, HEAD_DIM)).reshape(D_MODEL, N_KV_HEADS * LANES)
    return jnp.concatenate([wq, wk, wv], axis=1).astype(BF16)


def _trunk(x, gains, ffn_w, attn_w, attn_wo, sink, ret_w, ret_wo, lgf, lgb, attn_tabs, ret_tabs):
    batch, seq_len, _ = x.shape
    x = x.reshape(batch * seq_len, D_MODEL)
    for layer in range(2):
        g = gains[layer]
        x = _ffn(x, g[0], g[1], *ffn_w, layer, 0)
        if layer == 0:
            q, kd, vd = _attn_proj(x, g[2], attn_w, attn_tabs, seq_len)
            x = _attn_core(sink, q, kd, vd, x, g[3], attn_wo, batch, seq_len)
        else:
            proj = _ret_proj(x, g[2], ret_w, ret_tabs, seq_len)
            y = _ret_core(lgf, lgb, proj, batch, seq_len)
            x = _ret_out(y, proj, x, g[3], ret_wo)
        x = _ffn(x, g[4], g[5], *ffn_w, layer, 1)
    return x.reshape(batch, seq_len, D_MODEL)


def kernel(x_prompt, x_sample, norm_gains, ffn_w_in, ffn_w_out, attn_w_qkv, attn_w_o, attn_sink, ret_w_in, ret_w_o, ret_decay_fwd, ret_decay_bwd):
    gains = norm_gains.astype(F32).reshape(2, 6, 1, D_MODEL)
    ffn_w = (ffn_w_in.astype(BF16), ffn_w_out.astype(BF16))
    attn_w = _attn_weights(attn_w_qkv[0])
    attn_wo = attn_w_o[0].astype(BF16)
    sink = attn_sink[0].astype(F32) * LOG2_E
    ret_w = ret_w_in[0].astype(BF16)
    ret_wo = ret_w_o[0].astype(BF16)
    lgf = jax.nn.log_sigmoid(ret_decay_fwd[0].astype(F32))
    lgb = jax.nn.log_sigmoid(ret_decay_bwd[0].astype(F32))
    outs = []
    for x in (x_prompt, x_sample):
        seq_len = x.shape[1]
        outs.append(_trunk(x, gains, ffn_w, attn_w, attn_wo, sink, ret_w, ret_wo, lgf, lgb,
                           _attn_tables(seq_len), _ret_tables(seq_len)))
    return tuple(outs)
```

```python
import functools

import jax
import jax.numpy as jnp
import numpy as np
from jax import lax
from jax.experimental import pallas as pl
from jax.experimental.pallas import tpu as pltpu

D_MODEL = 1024
HEAD_DIM = 64
N_HEADS = 16
N_KV_HEADS = 4
GROUP = 4
ATT_BLOCK = 128
ROT_DIM = 16
ROPE_THETA = 500000.0
RET_HEADS = 4
RET_QK_DIM = 256
RET_V_DIM = 512
RET_V_TOTAL = 2048
RET_IN_DIM = 6144
RET_THETA = 10000.0
D_FF = 2816
NORM_EPS = 1e-6

LANES = 128
FF_TILE = 256
N_FF_TILES = D_FF // FF_TILE
TOKEN_TILE = 1024
RET_PROJ_TILE = 1024
RET_PROJ_SUB_BLOCKS = 4
PROJ_SUB_BLOCKS = 2
FFN_TOKEN_TILE = 1024
FFN_SUB_BLOCKS = 4
ATT_Q_TILE = 1024
ATT_LOOKAHEAD = 3
RET_BLOCK = 256
RET_UNROLL = 16
VMEM_LIMIT = 56 * 1024 * 1024

LOG2_E = 1.4426950408889634
ATT_Q_SCALE = HEAD_DIM ** -0.5 * LOG2_E

F32 = jnp.float32
BF16 = jnp.bfloat16


def _rms(x, g):
    return x * lax.rsqrt(jnp.mean(x * x, axis=-1, keepdims=True) + NORM_EPS) * g


def _const_spec(shape):
    zeros = (0,) * len(shape)
    return pl.BlockSpec(shape, lambda *_: zeros, pipeline_mode=pl.Buffered(1))


def _params(sem):
    return pltpu.CompilerParams(dimension_semantics=sem, vmem_limit_bytes=VMEM_LIMIT)


def _ffn_hidden(x, rows, gpre_ref, wi_ref, act_ref):
    h = _rms(x, gpre_ref[...]).astype(BF16)
    for f in range(N_FF_TILES):
        gate = jnp.dot(h, wi_ref[:, f * FF_TILE:(f + 1) * FF_TILE], preferred_element_type=F32)
        up = jnp.dot(h, wi_ref[:, D_FF + f * FF_TILE:D_FF + (f + 1) * FF_TILE], preferred_element_type=F32)
        act_ref[rows, f * FF_TILE:(f + 1) * FF_TILE] = (gate * jax.nn.sigmoid(gate) * up).astype(BF16)


def _ffn_kernel(x_ref, gpre_ref, gpost_ref, wi_ref, wo_ref, o_ref, act_ref):
    sub = x_ref.shape[0] // FFN_SUB_BLOCKS
    for s in range(FFN_SUB_BLOCKS):
        rows = slice(s * sub, (s + 1) * sub)
        _ffn_hidden(x_ref[rows, :], rows, gpre_ref, wi_ref, act_ref)
    for s in range(FFN_SUB_BLOCKS):
        rows = slice(s * sub, (s + 1) * sub)
        y = jnp.dot(act_ref[rows, :], wo_ref[...], preferred_element_type=F32)
        o_ref[rows, :] = x_ref[rows, :] + 0.5 * _rms(y, gpost_ref[...])


def _slab_spec(stacked, layer, slot):
    return pl.BlockSpec((None, None) + stacked.shape[2:], lambda *_: (layer, slot, 0, 0),
                        pipeline_mode=pl.Buffered(1))


def _ffn(x, g_pre, g_post, wi, wo, layer, slot):
    n = x.shape[0]
    tm = FFN_TOKEN_TILE
    row = pl.BlockSpec((tm, D_MODEL), lambda i: (i, 0))
    return pl.pallas_call(
        _ffn_kernel,
        grid=(n // tm,),
        in_specs=[row, _const_spec((1, D_MODEL)), _const_spec((1, D_MODEL)),
                  _slab_spec(wi, layer, slot), _slab_spec(wo, layer, slot)],
        out_specs=row,
        out_shape=jax.ShapeDtypeStruct((n, D_MODEL), F32),
        scratch_shapes=[pltpu.VMEM((tm, D_FF), BF16)],
        compiler_params=_params(("parallel",)),
        name="ffn",
    )(x, g_pre, g_post, wi, wo)


def _attn_proj_kernel(x_ref, g_ref, w_ref, c_ref, s1_ref, s2_ref, q_ref, k_ref, v_ref):
    n_q = N_HEADS * HEAD_DIM // LANES
    n_k = N_KV_HEADS
    sub = x_ref.shape[0] // PROJ_SUB_BLOCKS
    for s in range(PROJ_SUB_BLOCKS):
        rows = slice(s * sub, (s + 1) * sub)
        h = _rms(x_ref[rows, :], g_ref[...]).astype(BF16)
        qkv = jnp.dot(h, w_ref[...], preferred_element_type=F32)
        c, s1, s2 = c_ref[rows, :], s1_ref[rows, :], s2_ref[rows, :]
        for j in range(n_q + n_k):
            xj = qkv[:, j * LANES:(j + 1) * LANES]
            rot = xj * c + pltpu.roll(xj, LANES - ROT_DIM // 2, 1) * s1 + pltpu.roll(xj, ROT_DIM // 2, 1) * s2
            if j < n_q:
                q_ref[rows, j * LANES:(j + 1) * LANES] = (rot * ATT_Q_SCALE).astype(BF16)
            else:
                jk = j - n_q
                k_ref[rows, jk * LANES:(jk + 1) * LANES] = rot.astype(BF16)
        v_ref[rows, :] = qkv[:, (n_q + n_k) * LANES:].astype(BF16)


def _attn_proj(x, g, w, tabs, seq_len):
    n = x.shape[0]
    tm = TOKEN_TILE
    per_seq = seq_len // tm
    row = pl.BlockSpec((tm, D_MODEL), lambda i: (i, 0))
    tab = pl.BlockSpec((tm, LANES), lambda i: (i % per_seq, 0))
    kv_w = N_KV_HEADS * LANES
    return pl.pallas_call(
        _attn_proj_kernel,
        grid=(n // tm,),
        in_specs=[row, _const_spec((1, D_MODEL)), _const_spec(w.shape), tab, tab, tab],
        out_specs=[row, pl.BlockSpec((tm, kv_w), lambda i: (i, 0)), pl.BlockSpec((tm, kv_w), lambda i: (i, 0))],
        out_shape=[jax.ShapeDtypeStruct((n, D_MODEL), BF16), jax.ShapeDtypeStruct((n, kv_w), BF16),
                   jax.ShapeDtypeStruct((n, kv_w), BF16)],
        compiler_params=_params(("parallel",)),
        name="attn_proj",
    )(x, g, w, *tabs)


def _attn_core_kernel(sink_ref, q_ref, kp_ref, km_ref, kn_ref, vp_ref, vm_ref, vn_ref, x_ref, g_ref, wo_ref,
                      o_ref, kext_ref, vt_ref, obuf_ref, bias_ref, *, n_blocks):
    tq = q_ref.shape[0]
    blocks = tq // ATT_BLOCK
    i = pl.program_id(1)
    blk = ATT_BLOCK
    kext_ref[0:blk] = kp_ref[...]
    kext_ref[blk:blk + tq] = km_ref[...]
    kext_ref[blk + tq:] = kn_ref[...]
    vt_ref[:, 0:blk] = vp_ref[...].T
    vt_ref[:, blk:blk + tq] = vm_ref[...].T
    vt_ref[:, blk + tq:] = vn_ref[...].T

    heads_per_col = LANES // HEAD_DIM
    width = heads_per_col * blk
    n_cols = N_HEADS // heads_per_col
    key = lax.broadcasted_iota(jnp.int32, (blk, width), 0)
    qlane = lax.broadcasted_iota(jnp.int32, (blk, width), 1)
    qpos = qlane & (blk - 1)
    neg_inf = jnp.float32(-jnp.inf)
    bias_ref[0] = jnp.where(key >= qpos, 0.0, neg_inf)
    bias_ref[1] = jnp.where(key <= qpos, 0.0, neg_inf)
    lane = lax.broadcasted_iota(jnp.int32, (blk, LANES), 1)
    lo = lane < HEAD_DIM
    top = lax.broadcasted_iota(jnp.int32, (LANES, blk), 0) < HEAD_DIM
    first_head = lax.broadcasted_iota(jnp.int32, (1, width), 1) < blk
    sink_rows = [jnp.where(first_head, sink_ref[heads_per_col * c], sink_ref[heads_per_col * c + 1])
                 for c in range(n_cols)]

    def scores(qb, c):
        r0 = qb * blk
        g = c * heads_per_col // GROUP
        kg = kext_ref[r0:r0 + 3 * blk, g * LANES:(g + 1) * LANES]
        qp = q_ref[r0:r0 + blk, c * LANES:(c + 1) * LANES]
        zero = jnp.zeros_like(qp)
        qs = jnp.concatenate([jnp.where(lo, qp, zero), jnp.where(lo, zero, qp)], axis=0)
        return lax.dot_general(kg, qs, (((1,), (1,)), ((), ())), preferred_element_type=F32)

    def side_biases(qb):
        gb = i * blocks + qb
        return (bias_ref[0] + jnp.where(gb == 0, neg_inf, 0.0),
                bias_ref[1] + jnp.where(gb == n_blocks - 1, neg_inf, 0.0))

    def finish(qb, c, st, biases):
        r0 = qb * blk
        g = c * heads_per_col // GROUP
        s_prev = st[:blk] + biases[0]
        s_mid = st[blk:2 * blk]
        s_next = st[2 * blk:] + biases[1]
        m = jnp.max(jnp.maximum(jnp.maximum(s_prev, s_mid), s_next), axis=0, keepdims=True)
        m = jnp.maximum(m, sink_rows[c])
        p_prev, p_mid, p_next = jnp.exp2(s_prev - m), jnp.exp2(s_mid - m), jnp.exp2(s_next - m)
        denom = jnp.sum(p_prev + p_mid + p_next, axis=0, keepdims=True) + jnp.exp2(sink_rows[c] - m)
        pt = jnp.concatenate([p_prev, p_mid, p_next], axis=0).astype(BF16)
        vt = vt_ref[g * LANES:(g + 1) * LANES, r0:r0 + 3 * blk]
        ot = jnp.dot(vt, pt, preferred_element_type=F32) * (1.0 / denom)
        obuf_ref[r0:r0 + blk, c * LANES:(c + 1) * LANES] = jnp.where(top, ot[:, :blk], ot[:, blk:]).T.astype(BF16)

    tiles = [(qb, c) for qb in range(blocks) for c in range(n_cols)]
    pending = [scores(*tile) for tile in tiles[:ATT_LOOKAHEAD]]
    for t, (qb, c) in enumerate(tiles):
        if t + ATT_LOOKAHEAD < len(tiles):
            pending.append(scores(*tiles[t + ATT_LOOKAHEAD]))
        if c == 0:
            biases = side_biases(qb)
        finish(qb, c, pending.pop(0), biases)

    y = jnp.dot(obuf_ref[...], wo_ref[...], preferred_element_type=F32)
    o_ref[...] = x_ref[...] + _rms(y, g_ref[...])


def _attn_core(sink, q, kd, vd, x, g, wo, batch, seq_len):
    tq = ATT_Q_TILE
    blocks = tq // ATT_BLOCK
    n_blocks = seq_len // ATT_BLOCK
    kv_w = N_KV_HEADS * LANES
    q3 = q.reshape(batch, seq_len, D_MODEL)
    k3 = kd.reshape(batch, seq_len, kv_w)
    v3 = vd.reshape(batch, seq_len, kv_w)
    x3 = x.reshape(batch, seq_len, D_MODEL)
    row = pl.BlockSpec((None, tq, D_MODEL), lambda b, i: (b, i, 0))
    kv_main = pl.BlockSpec((None, tq, kv_w), lambda b, i: (b, i, 0))
    kv_prev = pl.BlockSpec((None, ATT_BLOCK, kv_w), lambda b, i: (b, jnp.maximum(i * blocks - 1, 0), 0))
    kv_next = pl.BlockSpec((None, ATT_BLOCK, kv_w), lambda b, i: (b, jnp.minimum((i + 1) * blocks, n_blocks - 1), 0))
    out = pl.pallas_call(
        functools.partial(_attn_core_kernel, n_blocks=n_blocks),
        grid=(batch, seq_len // tq),
        in_specs=[pl.BlockSpec(memory_space=pltpu.SMEM), row, kv_prev, kv_main, kv_next, kv_prev, kv_main, kv_next,
                  row, _const_spec((1, D_MODEL)), _const_spec(wo.shape)],
        out_specs=row,
        out_shape=jax.ShapeDtypeStruct((batch, seq_len, D_MODEL), F32),
        scratch_shapes=[pltpu.VMEM((tq + 2 * ATT_BLOCK, kv_w), BF16), pltpu.VMEM((kv_w, tq + 2 * ATT_BLOCK), BF16),
                        pltpu.VMEM((tq, D_MODEL), BF16),
                        pltpu.VMEM((2, ATT_BLOCK, LANES // HEAD_DIM * ATT_BLOCK), F32)],
        compiler_params=_params(("parallel", "parallel")),
        name="attn_core",
    )(sink, q3, k3, k3, k3, v3, v3, v3, x3, g, wo)
    return out.reshape(batch * seq_len, D_MODEL)


def _ret_proj_kernel(x_ref, g_ref, w_ref, cos_ref, sin_ref, o_ref):
    half = RET_QK_DIM // 2
    n_chunks = RET_IN_DIM // D_MODEL
    sub = x_ref.shape[0] // RET_PROJ_SUB_BLOCKS
    for s in range(RET_PROJ_SUB_BLOCKS):
        rows = slice(s * sub, (s + 1) * sub)
        h = _rms(x_ref[rows, :], g_ref[...]).astype(BF16)
        cos, sin = cos_ref[rows, :], sin_ref[rows, :]
        for c in range(n_chunks):
            y = jnp.dot(h, w_ref[:, c * D_MODEL:(c + 1) * D_MODEL], preferred_element_type=F32)
            if c < 2:
                scale = 1.0 if c == 0 else RET_QK_DIM ** -0.5
                for hd in range(RET_HEADS):
                    x1 = y[:, hd * RET_QK_DIM:hd * RET_QK_DIM + half]
                    x2 = y[:, hd * RET_QK_DIM + half:(hd + 1) * RET_QK_DIM]
                    base = c * D_MODEL + hd * RET_QK_DIM
                    o_ref[rows, base:base + half] = ((x1 * cos - x2 * sin) * scale).astype(BF16)
                    o_ref[rows, base + half:base + 2 * half] = ((x2 * cos + x1 * sin) * scale).astype(BF16)
            elif c < 4:
                o_ref[rows, c * D_MODEL:(c + 1) * D_MODEL] = y.astype(BF16)
            else:
                o_ref[rows, c * D_MODEL:(c + 1) * D_MODEL] = (y * jax.nn.sigmoid(y)).astype(BF16)


def _ret_proj(x, g, w, tabs, seq_len):
    n = x.shape[0]
    tm = RET_PROJ_TILE
    per_seq = seq_len // tm
    row = pl.BlockSpec((tm, D_MODEL), lambda i: (i, 0))
    tab = pl.BlockSpec((tm, LANES), lambda i: (i % per_seq, 0))
    return pl.pallas_call(
        _ret_proj_kernel,
        grid=(n // tm,),
        in_specs=[row, _const_spec((1, D_MODEL)), _const_spec(w.shape), tab, tab],
        out_specs=pl.BlockSpec((tm, RET_IN_DIM), lambda i: (i, 0)),
        out_shape=jax.ShapeDtypeStruct((n, RET_IN_DIM), BF16),
        compiler_params=_params(("parallel",)),
        name="ret_proj",
    )(x, g, w, *tabs)


def _ret_core_kernel(lgf_ref, lgb_ref, q_ref, k_ref, v_ref, o_ref, acc_ref, st_ref, dmat_ref):
    blk = RET_BLOCK
    seq_len = q_ref.shape[0]
    n_chunks = seq_len // blk
    hd = pl.program_id(1)
    lgf = lgf_ref[hd]
    lgb = lgb_ref[hd]
    ii = lax.broadcasted_iota(jnp.int32, (blk, blk), 0)
    jj = lax.broadcasted_iota(jnp.int32, (blk, blk), 1)
    diff = (ii - jj).astype(F32)
    dmat_ref[...] = jnp.where(diff >= 0, jnp.exp(lgf * jnp.maximum(diff, 0.0)),
                              jnp.exp(lgb * jnp.maximum(-diff, 0.0)))
    idx = lax.broadcasted_iota(jnp.int32, (blk, 1), 0).astype(F32)
    qdec_f = jnp.exp(lgf * (idx + 1.0))
    kdec_f = jnp.exp(lgf * (blk - 1.0 - idx))
    cdec_f = jnp.exp(lgf * blk)
    qdec_b = jnp.exp(lgb * (blk - idx))
    kdec_b = jnp.exp(lgb * idx)
    cdec_b = jnp.exp(lgb * blk)

    def state_delta(kc, vc, kdec):
        kd = (kc.astype(F32) * kdec).astype(BF16)
        return lax.dot_general(kd, vc, (((0,), (0,)), ((), ())), preferred_element_type=F32)

    st_ref[...] = jnp.zeros_like(st_ref)

    def fwd(c, carry):
        rows = pl.ds(pl.multiple_of(c * blk, blk), blk)
        qc, kc, vc = q_ref[rows, :], k_ref[rows, :], v_ref[rows, :]
        s = lax.dot_general(qc, kc, (((1,), (1,)), ((), ())), preferred_element_type=F32)
        delta = state_delta(kc, vc, kdec_f)
        cross = jnp.dot(qc, st_ref[...].astype(BF16), preferred_element_type=F32) * qdec_f
        intra = jnp.dot((s * dmat_ref[...]).astype(BF16), vc, preferred_element_type=F32)
        acc_ref[rows, :] = intra + cross
        st_ref[...] = cdec_f * st_ref[...] + delta
        return carry

    lax.fori_loop(0, n_chunks, fwd, 0, unroll=min(RET_UNROLL, n_chunks))

    st_ref[...] = jnp.zeros_like(st_ref)

    def bwd(t, carry):
        c = n_chunks - 1 - t
        rows = pl.ds(pl.multiple_of(c * blk, blk), blk)
        qc, kc, vc = q_ref[rows, :], k_ref[rows, :], v_ref[rows, :]
        delta = state_delta(kc, vc, kdec_b)
        cross = jnp.dot(qc, st_ref[...].astype(BF16), preferred_element_type=F32) * qdec_b
        o_ref[rows, :] = (acc_ref[rows, :] + cross).astype(BF16)
        st_ref[...] = cdec_b * st_ref[...] + delta
        return carry

    lax.fori_loop(0, n_chunks, bwd, 0, unroll=min(RET_UNROLL, n_chunks))


def _ret_core(lgf, lgb, proj, batch, seq_len):
    p3 = proj.reshape(batch, seq_len, RET_IN_DIM)
    qk_blocks = D_MODEL // RET_QK_DIM
    v_off = 2 * D_MODEL // RET_V_DIM
    smem = pl.BlockSpec(memory_space=pltpu.SMEM)
    out = pl.pallas_call(
        _ret_core_kernel,
        grid=(batch, RET_HEADS),
        in_specs=[smem, smem,
                  pl.BlockSpec((None, seq_len, RET_QK_DIM), lambda b, h: (b, 0, h)),
                  pl.BlockSpec((None, seq_len, RET_QK_DIM), lambda b, h: (b, 0, qk_blocks + h)),
                  pl.BlockSpec((None, seq_len, RET_V_DIM), lambda b, h: (b, 0, v_off + h))],
        out_specs=pl.BlockSpec((None, seq_len, RET_V_DIM), lambda b, h: (b, 0, h)),
        out_shape=jax.ShapeDtypeStruct((batch, seq_len, RET_V_TOTAL), BF16),
        scratch_shapes=[pltpu.VMEM((seq_len, RET_V_DIM), F32), pltpu.VMEM((RET_QK_DIM, RET_V_DIM), F32),
                        pltpu.VMEM((RET_BLOCK, RET_BLOCK), F32)],
        compiler_params=_params(("parallel", "parallel")),
        name="ret_core",
    )(lgf, lgb, p3, p3, p3)
    return out.reshape(batch * seq_len, RET_V_TOTAL)


def _ret_out_kernel(y_ref, gate_ref, x_ref, g_ref, w_ref, o_ref, z_ref):
    sub = x_ref.shape[0] // PROJ_SUB_BLOCKS
    for s in range(PROJ_SUB_BLOCKS):
        rows = slice(s * sub, (s + 1) * sub)
        for hd in range(RET_HEADS):
            cols = slice(hd * RET_V_DIM, (hd + 1) * RET_V_DIM)
            y = y_ref[rows, cols].astype(F32)
            yn = y * lax.rsqrt(jnp.mean(y * y, axis=-1, keepdims=True) + NORM_EPS)
            z_ref[rows, cols] = (gate_ref[rows, cols].astype(F32) * yn).astype(BF16)
        out = jnp.dot(z_ref[rows, :], w_ref[...], preferred_element_type=F32)
        o_ref[rows, :] = x_ref[rows, :] + _rms(out, g_ref[...])


def _ret_out(y, proj, x, g, w):
    n = x.shape[0]
    tm = TOKEN_TILE
    row = pl.BlockSpec((tm, D_MODEL), lambda i: (i, 0))
    wide = pl.BlockSpec((tm, RET_V_TOTAL), lambda i: (i, 0))
    gate_cols = pl.BlockSpec((tm, RET_V_TOTAL), lambda i: (i, RET_IN_DIM // RET_V_TOTAL - 1))
    return pl.pallas_call(
        _ret_out_kernel,
        grid=(n // tm,),
        in_specs=[wide, gate_cols, row, _const_spec((1, D_MODEL)), _const_spec(w.shape)],
        out_specs=row,
        out_shape=jax.ShapeDtypeStruct((n, D_MODEL), F32),
        scratch_shapes=[pltpu.VMEM((tm, RET_V_TOTAL), BF16)],
        compiler_params=_params(("parallel",)),
        name="ret_out",
    )(y, proj, x, g, w)


def _angles(seq_len, half, theta):
    inv_freq = theta ** (-np.arange(half, dtype=np.float64) / half)
    return np.arange(seq_len, dtype=np.float64)[:, None] * inv_freq[None, :]


def _attn_tables(seq_len):
    half = ROT_DIM // 2
    ang = _angles(seq_len, half, ROPE_THETA)
    cos, sin = np.cos(ang), np.sin(ang)
    pad = HEAD_DIM - ROT_DIM
    ones = np.ones((seq_len, pad))
    zeros = np.zeros((seq_len, pad))
    zh = np.zeros((seq_len, half))
    c = np.concatenate([cos, cos, ones], axis=1)
    s1 = np.concatenate([-sin, zh, zeros], axis=1)
    s2 = np.concatenate([zh, sin, zeros], axis=1)
    reps = LANES // HEAD_DIM
    return tuple(jnp.asarray(np.tile(t, (1, reps)), F32) for t in (c, s1, s2))


def _ret_tables(seq_len):
    ang = _angles(seq_len, RET_QK_DIM // 2, RET_THETA)
    return jnp.asarray(np.cos(ang), F32), jnp.asarray(np.sin(ang), F32)


def _attn_weights(w_qkv):
    q_dim = N_HEADS * HEAD_DIM
    kv_dim = N_KV_HEADS * HEAD_DIM
    wq = w_qkv[:, :q_dim]
    wk = w_qkv[:, q_dim:q_dim + kv_dim].reshape(D_MODEL, N_KV_HEADS, 1, HEAD_DIM)
    wv = w_qkv[:, q_dim + kv_dim:].reshape(D_MODEL, N_KV_HEADS, 1, HEAD_DIM)
    reps = LANES // HEAD_DIM
    wk = jnp.broadcast_to(wk, (D_MODEL, N_KV_HEADS, reps, HEAD_DIM)).reshape(D_MODEL, N_KV_HEADS * LANES)
    wv = jnp.broadcast_to(wv, (D_MODEL, N_KV_HEADS, reps, HEAD_DIM)).reshape(D_MODEL, N_KV_HEADS * LANES)
    return jnp.concatenate([wq, wk, wv], axis=1).astype(BF16)


def _trunk(x, gains, ffn_w, attn_w, attn_wo, sink, ret_w, ret_wo, lgf, lgb, attn_tabs, ret_tabs):
    batch, seq_len, _ = x.shape
    x = x.reshape(batch * seq_len, D_MODEL)
    for layer in range(2):
        g = gains[layer]
        x = _ffn(x, g[0], g[1], *ffn_w, layer, 0)
        if layer == 0:
            q, kd, vd = _attn_proj(x, g[2], attn_w, attn_tabs, seq_len)
            x = _attn_core(sink, q, kd, vd, x, g[3], attn_wo, batch, seq_len)
        else:
            proj = _ret_proj(x, g[2], ret_w, ret_tabs, seq_len)
            y = _ret_core(lgf, lgb, proj, batch, seq_len)
            x = _ret_out(y, proj, x, g[3], ret_wo)
        x = _ffn(x, g[4], g[5], *ffn_w, layer, 1)
    return x.reshape(batch, seq_len, D_MODEL)


def kernel(x_prompt, x_sample, norm_gains, ffn_w_in, ffn_w_out, attn_w_qkv, attn_w_o, attn_sink, ret_w_in, ret_w_o, ret_decay_fwd, ret_decay_bwd):
    gains = norm_gains.astype(F32).reshape(2, 6, 1, D_MODEL)
    ffn_w = (ffn_w_in.astype(BF16), ffn_w_out.astype(BF16))
    attn_w = _attn_weights(attn_w_qkv[0])
    attn_wo = attn_w_o[0].astype(BF16)
    sink = attn_sink[0].astype(F32) * LOG2_E
    ret_w = ret_w_in[0].astype(BF16)
    ret_wo = ret_w_o[0].astype(BF16)
    lgf = jax.nn.log_sigmoid(ret_decay_fwd[0].astype(F32))
    lgb = jax.nn.log_sigmoid(ret_decay_bwd[0].astype(F32))
    outs = []
    for x in (x_prompt, x_sample):
        seq_len = x.shape[1]
        outs.append(_trunk(x, gains, ffn_w, attn_w, attn_wo, sink, ret_w, ret_wo, lgf, lgb,
                           _attn_tables(seq_len), _ret_tables(seq_len)))
    return tuple(outs)
```

```python
import functools

import jax
import jax.numpy as jnp
import numpy as np
from jax import lax
from jax.experimental import pallas as pl
from jax.experimental.pallas import tpu as pltpu

D_MODEL = 1024
HEAD_DIM = 64
N_HEADS = 16
N_KV_HEADS = 4
GROUP = 4
ATT_BLOCK = 128
ROT_DIM = 16
ROPE_THETA = 500000.0
RET_HEADS = 4
RET_QK_DIM = 256
RET_V_DIM = 512
RET_V_TOTAL = 2048
RET_IN_DIM = 6144
RET_THETA = 10000.0
D_FF = 2816
NORM_EPS = 1e-6

LANES = 128
FF_TILE = 256
N_FF_TILES = D_FF // FF_TILE
TOKEN_TILE = 1024
RET_PROJ_TILE = 1024
RET_PROJ_SUB_BLOCKS = 4
PROJ_SUB_BLOCKS = 2
FFN_TOKEN_TILE = 512
FFN_SUB_BLOCKS = 2
ATT_Q_TILE = 1024
ATT_LOOKAHEAD = 3
RET_BLOCK = 256
RET_UNROLL = 16
VMEM_LIMIT = 56 * 1024 * 1024

LOG2_E = 1.4426950408889634
ATT_Q_SCALE = HEAD_DIM ** -0.5 * LOG2_E

F32 = jnp.float32
BF16 = jnp.bfloat16


def _rms(x, g):
    return x * lax.rsqrt(jnp.mean(x * x, axis=-1, keepdims=True) + NORM_EPS) * g


def _const_spec(shape):
    zeros = (0,) * len(shape)
    return pl.BlockSpec(shape, lambda *_: zeros, pipeline_mode=pl.Buffered(1))


def _params(sem):
    return pltpu.CompilerParams(dimension_semantics=sem, vmem_limit_bytes=VMEM_LIMIT)


def _ffn_hidden(x, rows, gpre_ref, wi_ref, act_ref):
    h = _rms(x, gpre_ref[...]).astype(BF16)
    for f in range(N_FF_TILES):
        gate = jnp.dot(h, wi_ref[:, f * FF_TILE:(f + 1) * FF_TILE], preferred_element_type=F32)
        up = jnp.dot(h, wi_ref[:, D_FF + f * FF_TILE:D_FF + (f + 1) * FF_TILE], preferred_element_type=F32)
        act_ref[rows, f * FF_TILE:(f + 1) * FF_TILE] = (gate * jax.nn.sigmoid(gate) * up).astype(BF16)


def _ffn_kernel(x_ref, gpre_ref, gpost_ref, wi_ref, wo_ref, o_ref, act_ref):
    sub = x_ref.shape[0] // FFN_SUB_BLOCKS
    for s in range(FFN_SUB_BLOCKS):
        rows = slice(s * sub, (s + 1) * sub)
        _ffn_hidden(x_ref[rows, :], rows, gpre_ref, wi_ref, act_ref)
    for s in range(FFN_SUB_BLOCKS):
        rows = slice(s * sub, (s + 1) * sub)
        y = jnp.dot(act_ref[rows, :], wo_ref[...], preferred_element_type=F32)
        o_ref[rows, :] = x_ref[rows, :] + 0.5 * _rms(y, gpost_ref[...])


def _slab_spec(stacked, layer, slot):
    return pl.BlockSpec((None, None) + stacked.shape[2:], lambda *_: (layer, slot, 0, 0),
                        pipeline_mode=pl.Buffered(1))


def _ffn(x, g_pre, g_post, wi, wo, layer, slot):
    n = x.shape[0]
    tm = FFN_TOKEN_TILE
    row = pl.BlockSpec((tm, D_MODEL), lambda i: (i, 0))
    return pl.pallas_call(
        _ffn_kernel,
        grid=(n // tm,),
        in_specs=[row, _const_spec((1, D_MODEL)), _const_spec((1, D_MODEL)),
                  _slab_spec(wi, layer, slot), _slab_spec(wo, layer, slot)],
        out_specs=row,
        out_shape=jax.ShapeDtypeStruct((n, D_MODEL), F32),
        scratch_shapes=[pltpu.VMEM((tm, D_FF), BF16)],
        compiler_params=_params(("parallel",)),
        name="ffn",
    )(x, g_pre, g_post, wi, wo)


def _attn_proj_kernel(x_ref, g_ref, w_ref, c_ref, s1_ref, s2_ref, q_ref, k_ref, v_ref):
    n_q = N_HEADS * HEAD_DIM // LANES
    n_k = N_KV_HEADS
    sub = x_ref.shape[0] // PROJ_SUB_BLOCKS
    for s in range(PROJ_SUB_BLOCKS):
        rows = slice(s * sub, (s + 1) * sub)
        h = _rms(x_ref[rows, :], g_ref[...]).astype(BF16)
        qkv = jnp.dot(h, w_ref[...], preferred_element_type=F32)
        c, s1, s2 = c_ref[rows, :], s1_ref[rows, :], s2_ref[rows, :]
        for j in range(n_q + n_k):
            xj = qkv[:, j * LANES:(j + 1) * LANES]
            rot = xj * c + pltpu.roll(xj, LANES - ROT_DIM // 2, 1) * s1 + pltpu.roll(xj, ROT_DIM // 2, 1) * s2
            if j < n_q:
                q_ref[rows, j * LANES:(j + 1) * LANES] = (rot * ATT_Q_SCALE).astype(BF16)
            else:
                jk = j - n_q
                k_ref[rows, jk * LANES:(jk + 1) * LANES] = rot.astype(BF16)
        v_ref[rows, :] = qkv[:, (n_q + n_k) * LANES:].astype(BF16)


def _attn_proj(x, g, w, tabs, seq_len):
    n = x.shape[0]
    tm = TOKEN_TILE
    per_seq = seq_len // tm
    row = pl.BlockSpec((tm, D_MODEL), lambda i: (i, 0))
    tab = pl.BlockSpec((tm, LANES), lambda i: (i % per_seq, 0))
    kv_w = N_KV_HEADS * LANES
    return pl.pallas_call(
        _attn_proj_kernel,
        grid=(n // tm,),
        in_specs=[row, _const_spec((1, D_MODEL)), _const_spec(w.shape), tab, tab, tab],
        out_specs=[row, pl.BlockSpec((tm, kv_w), lambda i: (i, 0)), pl.BlockSpec((tm, kv_w), lambda i: (i, 0))],
        out_shape=[jax.ShapeDtypeStruct((n, D_MODEL), BF16), jax.ShapeDtypeStruct((n, kv_w), BF16),
                   jax.ShapeDtypeStruct((n, kv_w), BF16)],
        compiler_params=_params(("parallel",)),
        name="attn_proj",
    )(x, g, w, *tabs)


def _attn_core_kernel(sink_ref, q_ref, kp_ref, km_ref, kn_ref, vp_ref, vm_ref, vn_ref, x_ref, g_ref, wo_ref,
                      o_ref, kext_ref, vt_ref, obuf_ref, bias_ref, *, n_blocks):
    tq = q_ref.shape[0]
    blocks = tq // ATT_BLOCK
    i = pl.program_id(1)
    blk = ATT_BLOCK
    kext_ref[0:blk] = kp_ref[...]
    kext_ref[blk:blk + tq] = km_ref[...]
    kext_ref[blk + tq:] = kn_ref[...]
    vt_ref[:, 0:blk] = vp_ref[...].T
    vt_ref[:, blk:blk + tq] = vm_ref[...].T
    vt_ref[:, blk + tq:] = vn_ref[...].T

    heads_per_col = LANES // HEAD_DIM
    width = heads_per_col * blk
    n_cols = N_HEADS // heads_per_col
    key = lax.broadcasted_iota(jnp.int32, (blk, width), 0)
    qlane = lax.broadcasted_iota(jnp.int32, (blk, width), 1)
    qpos = qlane & (blk - 1)
    neg_inf = jnp.float32(-jnp.inf)
    bias_ref[0] = jnp.where(key >= qpos, 0.0, neg_inf)
    bias_ref[1] = jnp.where(key <= qpos, 0.0, neg_inf)
    lane = lax.broadcasted_iota(jnp.int32, (blk, LANES), 1)
    lo = lane < HEAD_DIM
    top = lax.broadcasted_iota(jnp.int32, (LANES, blk), 0) < HEAD_DIM
    first_head = lax.broadcasted_iota(jnp.int32, (1, width), 1) < blk
    sink_rows = [jnp.where(first_head, sink_ref[heads_per_col * c], sink_ref[heads_per_col * c + 1])
                 for c in range(n_cols)]

    def scores(qb, c):
        r0 = qb * blk
        g = c * heads_per_col // GROUP
        kg = kext_ref[r0:r0 + 3 * blk, g * LANES:(g + 1) * LANES]
        qp = q_ref[r0:r0 + blk, c * LANES:(c + 1) * LANES]
        zero = jnp.zeros_like(qp)
        qs = jnp.concatenate([jnp.where(lo, qp, zero), jnp.where(lo, zero, qp)], axis=0)
        return lax.dot_general(kg, qs, (((1,), (1,)), ((), ())), preferred_element_type=F32)

    def finish(qb, c, st):
        gb = i * blocks + qb
        r0 = qb * blk
        g = c * heads_per_col // GROUP
        s_prev = st[:blk] + (bias_ref[0] + jnp.where(gb == 0, neg_inf, 0.0))
        s_mid = st[blk:2 * blk]
        s_next = st[2 * blk:] + (bias_ref[1] + jnp.where(gb == n_blocks - 1, neg_inf, 0.0))
        m = jnp.max(jnp.maximum(jnp.maximum(s_prev, s_mid), s_next), axis=0, keepdims=True)
        m = jnp.maximum(m, sink_rows[c])
        p_prev, p_mid, p_next = jnp.exp2(s_prev - m), jnp.exp2(s_mid - m), jnp.exp2(s_next - m)
        denom = jnp.sum(p_prev + p_mid + p_next, axis=0, keepdims=True) + jnp.exp2(sink_rows[c] - m)
        pt = jnp.concatenate([p_prev, p_mid, p_next], axis=0).astype(BF16)
        vt = vt_ref[g * LANES:(g + 1) * LANES, r0:r0 + 3 * blk]
        ot = jnp.dot(vt, pt, preferred_element_type=F32) * (1.0 / denom)
        obuf_ref[r0:r0 + blk, c * LANES:(c + 1) * LANES] = jnp.where(top, ot[:, :blk], ot[:, blk:]).T.astype(BF16)

    tiles = [(qb, c) for qb in range(blocks) for c in range(n_cols)]
    pending = [scores(*tile) for tile in tiles[:ATT_LOOKAHEAD]]
    for t, (qb, c) in enumerate(tiles):
        if t + ATT_LOOKAHEAD < len(tiles):
            pending.append(scores(*tiles[t + ATT_LOOKAHEAD]))
        finish(qb, c, pending.pop(0))

    y = jnp.dot(obuf_ref[...], wo_ref[...], preferred_element_type=F32)
    o_ref[...] = x_ref[...] + _rms(y, g_ref[...])


def _attn_core(sink, q, kd, vd, x, g, wo, batch, seq_len):
    tq = ATT_Q_TILE
    blocks = tq // ATT_BLOCK
    n_blocks = seq_len // ATT_BLOCK
    kv_w = N_KV_HEADS * LANES
    q3 = q.reshape(batch, seq_len, D_MODEL)
    k3 = kd.reshape(batch, seq_len, kv_w)
    v3 = vd.reshape(batch, seq_len, kv_w)
    x3 = x.reshape(batch, seq_len, D_MODEL)
    row = pl.BlockSpec((None, tq, D_MODEL), lambda b, i: (b, i, 0))
    kv_main = pl.BlockSpec((None, tq, kv_w), lambda b, i: (b, i, 0))
    kv_prev = pl.BlockSpec((None, ATT_BLOCK, kv_w), lambda b, i: (b, jnp.maximum(i * blocks - 1, 0), 0))
    kv_next = pl.BlockSpec((None, ATT_BLOCK, kv_w), lambda b, i: (b, jnp.minimum((i + 1) * blocks, n_blocks - 1), 0))
    out = pl.pallas_call(
        functools.partial(_attn_core_kernel, n_blocks=n_blocks),
        grid=(batch, seq_len // tq),
        in_specs=[pl.BlockSpec(memory_space=pltpu.SMEM), row, kv_prev, kv_main, kv_next, kv_prev, kv_main, kv_next,
                  row, _const_spec((1, D_MODEL)), _const_spec(wo.shape)],
        out_specs=row,
        out_shape=jax.ShapeDtypeStruct((batch, seq_len, D_MODEL), F32),
        scratch_shapes=[pltpu.VMEM((tq + 2 * ATT_BLOCK, kv_w), BF16), pltpu.VMEM((kv_w, tq + 2 * ATT_BLOCK), BF16),
                        pltpu.VMEM((tq, D_MODEL), BF16),
                        pltpu.VMEM((2, ATT_BLOCK, LANES // HEAD_DIM * ATT_BLOCK), F32)],
        compiler_params=_params(("parallel", "parallel")),
        name="attn_core",
    )(sink, q3, k3, k3, k3, v3, v3, v3, x3, g, wo)
    return out.reshape(batch * seq_len, D_MODEL)


def _ret_proj_kernel(x_ref, g_ref, w_ref, cos_ref, sin_ref, o_ref):
    half = RET_QK_DIM // 2
    n_chunks = RET_IN_DIM // D_MODEL
    sub = x_ref.shape[0] // RET_PROJ_SUB_BLOCKS
    for s in range(RET_PROJ_SUB_BLOCKS):
        rows = slice(s * sub, (s + 1) * sub)
        h = _rms(x_ref[rows, :], g_ref[...]).astype(BF16)
        cos, sin = cos_ref[rows, :], sin_ref[rows, :]
        for c in range(n_chunks):
            y = jnp.dot(h, w_ref[:, c * D_MODEL:(c + 1) * D_MODEL], preferred_element_type=F32)
            if c < 2:
                scale = 1.0 if c == 0 else RET_QK_DIM ** -0.5
                for hd in range(RET_HEADS):
                    x1 = y[:, hd * RET_QK_DIM:hd * RET_QK_DIM + half]
                    x2 = y[:, hd * RET_QK_DIM + half:(hd + 1) * RET_QK_DIM]
                    base = c * D_MODEL + hd * RET_QK_DIM
                    o_ref[rows, base:base + half] = ((x1 * cos - x2 * sin) * scale).astype(BF16)
                    o_ref[rows, base + half:base + 2 * half] = ((x2 * cos + x1 * sin) * scale).astype(BF16)
            elif c < 4:
                o_ref[rows, c * D_MODEL:(c + 1) * D_MODEL] = y.astype(BF16)
            else:
                o_ref[rows, c * D_MODEL:(c + 1) * D_MODEL] = (y * jax.nn.sigmoid(y)).astype(BF16)


def _ret_proj(x, g, w, tabs, seq_len):
    n = x.shape[0]
    tm = RET_PROJ_TILE
    per_seq = seq_len // tm
    row = pl.BlockSpec((tm, D_MODEL), lambda i: (i, 0))
    tab = pl.BlockSpec((tm, LANES), lambda i: (i % per_seq, 0))
    return pl.pallas_call(
        _ret_proj_kernel,
        grid=(n // tm,),
        in_specs=[row, _const_spec((1, D_MODEL)), _const_spec(w.shape), tab, tab],
        out_specs=pl.BlockSpec((tm, RET_IN_DIM), lambda i: (i, 0)),
        out_shape=jax.ShapeDtypeStruct((n, RET_IN_DIM), BF16),
        compiler_params=_params(("parallel",)),
        name="ret_proj",
    )(x, g, w, *tabs)


def _ret_core_kernel(lgf_ref, lgb_ref, q_ref, k_ref, v_ref, o_ref, acc_ref, st_ref, dmat_ref):
    blk = RET_BLOCK
    seq_len = q_ref.shape[0]
    n_chunks = seq_len // blk
    hd = pl.program_id(1)
    lgf = lgf_ref[hd]
    lgb = lgb_ref[hd]
    ii = lax.broadcasted_iota(jnp.int32, (blk, blk), 0)
    jj = lax.broadcasted_iota(jnp.int32, (blk, blk), 1)
    diff = (ii - jj).astype(F32)
    dmat_ref[...] = jnp.where(diff >= 0, jnp.exp(lgf * jnp.maximum(diff, 0.0)),
                              jnp.exp(lgb * jnp.maximum(-diff, 0.0)))
    idx = lax.broadcasted_iota(jnp.int32, (blk, 1), 0).astype(F32)
    qdec_f = jnp.exp(lgf * (idx + 1.0))
    kdec_f = jnp.exp(lgf * (blk - 1.0 - idx))
    cdec_f = jnp.exp(lgf * blk)
    qdec_b = jnp.exp(lgb * (blk - idx))
    kdec_b = jnp.exp(lgb * idx)
    cdec_b = jnp.exp(lgb * blk)

    def state_delta(kc, vc, kdec):
        kd = (kc.astype(F32) * kdec).astype(BF16)
        return lax.dot_general(kd, vc, (((0,), (0,)), ((), ())), preferred_element_type=F32)

    st_ref[...] = jnp.zeros_like(st_ref)

    def fwd(c, carry):
        rows = pl.ds(pl.multiple_of(c * blk, blk), blk)
        qc, kc, vc = q_ref[rows, :], k_ref[rows, :], v_ref[rows, :]
        s = lax.dot_general(qc, kc, (((1,), (1,)), ((), ())), preferred_element_type=F32)
        delta = state_delta(kc, vc, kdec_f)
        cross = jnp.dot(qc, st_ref[...].astype(BF16), preferred_element_type=F32) * qdec_f
        intra = jnp.dot((s * dmat_ref[...]).astype(BF16), vc, preferred_element_type=F32)
        acc_ref[rows, :] = intra + cross
        st_ref[...] = cdec_f * st_ref[...] + delta
        return carry

    lax.fori_loop(0, n_chunks, fwd, 0, unroll=min(RET_UNROLL, n_chunks))

    st_ref[...] = jnp.zeros_like(st_ref)

    def bwd(t, carry):
        c = n_chunks - 1 - t
        rows = pl.ds(pl.multiple_of(c * blk, blk), blk)
        qc, kc, vc = q_ref[rows, :], k_ref[rows, :], v_ref[rows, :]
        delta = state_delta(kc, vc, kdec_b)
        cross = jnp.dot(qc, st_ref[...].astype(BF16), preferred_element_type=F32) * qdec_b
        o_ref[rows, :] = (acc_ref[rows, :] + cross).astype(BF16)
        st_ref[...] = cdec_b * st_ref[...] + delta
        return carry

    lax.fori_loop(0, n_chunks, bwd, 0, unroll=min(RET_UNROLL, n_chunks))


def _ret_core(lgf, lgb, proj, batch, seq_len):
    p3 = proj.reshape(batch, seq_len, RET_IN_DIM)
    qk_blocks = D_MODEL // RET_QK_DIM
    v_off = 2 * D_MODEL // RET_V_DIM
    smem = pl.BlockSpec(memory_space=pltpu.SMEM)
    out = pl.pallas_call(
        _ret_core_kernel,
        grid=(batch, RET_HEADS),
        in_specs=[smem, smem,
                  pl.BlockSpec((None, seq_len, RET_QK_DIM), lambda b, h: (b, 0, h)),
                  pl.BlockSpec((None, seq_len, RET_QK_DIM), lambda b, h: (b, 0, qk_blocks + h)),
                  pl.BlockSpec((None, seq_len, RET_V_DIM), lambda b, h: (b, 0, v_off + h))],
        out_specs=pl.BlockSpec((None, seq_len, RET_V_DIM), lambda b, h: (b, 0, h)),
        out_shape=jax.ShapeDtypeStruct((batch, seq_len, RET_V_TOTAL), BF16),
        scratch_shapes=[pltpu.VMEM((seq_len, RET_V_DIM), F32), pltpu.VMEM((RET_QK_DIM, RET_V_DIM), F32),
                        pltpu.VMEM((RET_BLOCK, RET_BLOCK), F32)],
        compiler_params=_params(("parallel", "parallel")),
        name="ret_core",
    )(lgf, lgb, p3, p3, p3)
    return out.reshape(batch * seq_len, RET_V_TOTAL)


def _ret_out_kernel(y_ref, gate_ref, x_ref, g_ref, w_ref, o_ref, z_ref):
    sub = x_ref.shape[0] // PROJ_SUB_BLOCKS
    for s in range(PROJ_SUB_BLOCKS):
        rows = slice(s * sub, (s + 1) * sub)
        for hd in range(RET_HEADS):
            cols = slice(hd * RET_V_DIM, (hd + 1) * RET_V_DIM)
            y = y_ref[rows, cols].astype(F32)
            yn = y * lax.rsqrt(jnp.mean(y * y, axis=-1, keepdims=True) + NORM_EPS)
            z_ref[rows, cols] = (gate_ref[rows, cols].astype(F32) * yn).astype(BF16)
        out = jnp.dot(z_ref[rows, :], w_ref[...], preferred_element_type=F32)
        o_ref[rows, :] = x_ref[rows, :] + _rms(out, g_ref[...])


def _ret_out(y, proj, x, g, w):
    n = x.shape[0]
    tm = TOKEN_TILE
    row = pl.BlockSpec((tm, D_MODEL), lambda i: (i, 0))
    wide = pl.BlockSpec((tm, RET_V_TOTAL), lambda i: (i, 0))
    gate_cols = pl.BlockSpec((tm, RET_V_TOTAL), lambda i: (i, RET_IN_DIM // RET_V_TOTAL - 1))
    return pl.pallas_call(
        _ret_out_kernel,
        grid=(n // tm,),
        in_specs=[wide, gate_cols, row, _const_spec((1, D_MODEL)), _const_spec(w.shape)],
        out_specs=row,
        out_shape=jax.ShapeDtypeStruct((n, D_MODEL), F32),
        scratch_shapes=[pltpu.VMEM((tm, RET_V_TOTAL), BF16)],
        compiler_params=_params(("parallel",)),
        name="ret_out",
    )(y, proj, x, g, w)


def _angles(seq_len, half, theta):
    inv_freq = theta ** (-np.arange(half, dtype=np.float64) / half)
    return np.arange(seq_len, dtype=np.float64)[:, None] * inv_freq[None, :]


def _attn_tables(seq_len):
    half = ROT_DIM // 2
    ang = _angles(seq_len, half, ROPE_THETA)
    cos, sin = np.cos(ang), np.sin(ang)
    pad = HEAD_DIM - ROT_DIM
    ones = np.ones((seq_len, pad))
    zeros = np.zeros((seq_len, pad))
    zh = np.zeros((seq_len, half))
    c = np.concatenate([cos, cos, ones], axis=1)
    s1 = np.concatenate([-sin, zh, zeros], axis=1)
    s2 = np.concatenate([zh, sin, zeros], axis=1)
    reps = LANES // HEAD_DIM
    return tuple(jnp.asarray(np.tile(t, (1, reps)), F32) for t in (c, s1, s2))


def _ret_tables(seq_len):
    ang = _angles(seq_len, RET_QK_DIM // 2, RET_THETA)
    return jnp.asarray(np.cos(ang), F32), jnp.asarray(np.sin(ang), F32)


def _attn_weights(w_qkv):
    q_dim = N_HEADS * HEAD_DIM
    kv_dim = N_KV_HEADS * HEAD_DIM
    wq = w_qkv[:, :q_dim]
    wk = w_qkv[:, q_dim:q_dim + kv_dim].reshape(D_MODEL, N_KV_HEADS, 1, HEAD_DIM)
    wv = w_qkv[:, q_dim + kv_dim:].reshape(D_MODEL, N_KV_HEADS, 1, HEAD_DIM)
    reps = LANES // HEAD_DIM
    wk = jnp.broadcast_to(wk, (D_MODEL, N_KV_HEADS, reps, HEAD_DIM)).reshape(D_MODEL, N_KV_HEADS * LANES)
    wv = jnp.broadcast_to(wv, (D_MODEL, N_KV_HEADS, reps, HEAD_DIM)).reshape(D_MODEL, N_KV_HEADS * LANES)
    return jnp.concatenate([wq, wk, wv], axis=1).astype(BF16)


def _trunk(x, gains, ffn_w, attn_w, attn_wo, sink, ret_w, ret_wo, lgf, lgb, attn_tabs, ret_tabs):
    batch, seq_len, _ = x.shape
    x = x.reshape(batch * seq_len, D_MODEL)
    for layer in range(2):
        g = gains[layer]
        x = _ffn(x, g[0], g[1], *ffn_w, layer, 0)
        if layer == 0:
            q, kd, vd = _attn_proj(x, g[2], attn_w, attn_tabs, seq_len)
            x = _attn_core(sink, q, kd, vd, x, g[3], attn_wo, batch, seq_len)
        else:
            proj = _ret_proj(x, g[2], ret_w, ret_tabs, seq_len)
            y = _ret_core(lgf, lgb, proj, batch, seq_len)
            x = _ret_out(y, proj, x, g[3], ret_wo)
        x = _ffn(x, g[4], g[5], *ffn_w, layer, 1)
    return x.reshape(batch, seq_len, D_MODEL)


def kernel(x_prompt, x_sample, norm_gains, ffn_w_in, ffn_w_out, attn_w_qkv, attn_w_o, attn_sink, ret_w_in, ret_w_o, ret_decay_fwd, ret_decay_bwd):
    gains = norm_gains.astype(F32).reshape(2, 6, 1, D_MODEL)
    ffn_w = (ffn_w_in.astype(BF16), ffn_w_out.astype(BF16))
    attn_w = _attn_weights(attn_w_qkv[0])
    attn_wo = attn_w_o[0].astype(BF16)
    sink = attn_sink[0].astype(F32) * LOG2_E
    ret_w = ret_w_in[0].astype(BF16)
    ret_wo = ret_w_o[0].astype(BF16)
    lgf = jax.nn.log_sigmoid(ret_decay_fwd[0].astype(F32))
    lgb = jax.nn.log_sigmoid(ret_decay_bwd[0].astype(F32))
    outs = []
    for x in (x_prompt, x_sample):
        seq_len = x.shape[1]
        outs.append(_trunk(x, gains, ffn_w, attn_w, attn_wo, sink, ret_w, ret_wo, lgf, lgb,
                           _attn_tables(seq_len), _ret_tables(seq_len)))
    return tuple(outs)
```

```python
import functools

import jax
import jax.numpy as jnp
import numpy as np
from jax import lax
from jax.experimental import pallas as pl
from jax.experimental.pallas import tpu as pltpu

D_MODEL = 1024
HEAD_DIM = 64
N_HEADS = 16
N_KV_HEADS = 4
GROUP = 4
ATT_BLOCK = 128
ROT_DIM = 16
ROPE_THETA = 500000.0
RET_HEADS = 4
RET_QK_DIM = 256
RET_V_DIM = 512
RET_V_TOTAL = 2048
RET_IN_DIM = 6144
RET_QKV_DIM = 4096
RET_THETA = 10000.0
D_FF = 2816
NORM_EPS = 1e-6

LANES = 128
FF_TILE = 256
N_FF_TILES = D_FF // FF_TILE
TOKEN_TILE = 1024
RET_PROJ_TILE = 512
PROJ_SUB_BLOCKS = 2
FFN_TOKEN_TILE = 1024
FFN_SUB_BLOCKS = 4
ATT_Q_TILE = 1024
ATT_LOOKAHEAD = 3
RET_BLOCK = 256
RET_UNROLL = 16
VMEM_LIMIT = 56 * 1024 * 1024

LOG2_E = 1.4426950408889634
ATT_Q_SCALE = HEAD_DIM ** -0.5 * LOG2_E

F32 = jnp.float32
BF16 = jnp.bfloat16


def _rms(x, g):
    return x * lax.rsqrt(jnp.mean(x * x, axis=-1, keepdims=True) + NORM_EPS) * g


def _const_spec(shape):
    zeros = (0,) * len(shape)
    return pl.BlockSpec(shape, lambda *_: zeros, pipeline_mode=pl.Buffered(1))


def _params(sem):
    return pltpu.CompilerParams(dimension_semantics=sem, vmem_limit_bytes=VMEM_LIMIT)


def _ffn_hidden(x, rows, gpre_ref, wi_ref, act_ref):
    h = _rms(x, gpre_ref[...]).astype(BF16)
    for f in range(N_FF_TILES):
        gate = jnp.dot(h, wi_ref[:, f * FF_TILE:(f + 1) * FF_TILE], preferred_element_type=F32)
        up = jnp.dot(h, wi_ref[:, D_FF + f * FF_TILE:D_FF + (f + 1) * FF_TILE], preferred_element_type=F32)
        act_ref[rows, f * FF_TILE:(f + 1) * FF_TILE] = (gate * jax.nn.sigmoid(gate) * up).astype(BF16)


def _ffn_kernel(x_ref, gpre_ref, gpost_ref, wi_ref, wo_ref, o_ref, act_ref):
    sub = x_ref.shape[0] // FFN_SUB_BLOCKS
    for s in range(FFN_SUB_BLOCKS):
        rows = slice(s * sub, (s + 1) * sub)
        _ffn_hidden(x_ref[rows, :], rows, gpre_ref, wi_ref, act_ref)
    for s in range(FFN_SUB_BLOCKS):
        rows = slice(s * sub, (s + 1) * sub)
        y = jnp.dot(act_ref[rows, :], wo_ref[...], preferred_element_type=F32)
        o_ref[rows, :] = x_ref[rows, :] + 0.5 * _rms(y, gpost_ref[...])


def _slab_spec(stacked, layer, slot):
    return pl.BlockSpec((None, None) + stacked.shape[2:], lambda *_: (layer, slot, 0, 0),
                        pipeline_mode=pl.Buffered(1))


def _ffn(x, g_pre, g_post, wi, wo, layer, slot):
    n = x.shape[0]
    tm = FFN_TOKEN_TILE
    row = pl.BlockSpec((tm, D_MODEL), lambda i: (i, 0))
    return pl.pallas_call(
        _ffn_kernel,
        grid=(n // tm,),
        in_specs=[row, _const_spec((1, D_MODEL)), _const_spec((1, D_MODEL)),
                  _slab_spec(wi, layer, slot), _slab_spec(wo, layer, slot)],
        out_specs=row,
        out_shape=jax.ShapeDtypeStruct((n, D_MODEL), F32),
        scratch_shapes=[pltpu.VMEM((tm, D_FF), BF16)],
        compiler_params=_params(("parallel",)),
        name="ffn",
    )(x, g_pre, g_post, wi, wo)


def _attn_proj_kernel(x_ref, g_ref, w_ref, c_ref, s1_ref, s2_ref, q_ref, k_ref, v_ref):
    n_q = N_HEADS * HEAD_DIM // LANES
    n_k = N_KV_HEADS
    sub = x_ref.shape[0] // PROJ_SUB_BLOCKS
    for s in range(PROJ_SUB_BLOCKS):
        rows = slice(s * sub, (s + 1) * sub)
        h = _rms(x_ref[rows, :], g_ref[...]).astype(BF16)
        qkv = jnp.dot(h, w_ref[...], preferred_element_type=F32)
        c, s1, s2 = c_ref[rows, :], s1_ref[rows, :], s2_ref[rows, :]
        for j in range(n_q + n_k):
            xj = qkv[:, j * LANES:(j + 1) * LANES]
            rot = xj * c + pltpu.roll(xj, LANES - ROT_DIM // 2, 1) * s1 + pltpu.roll(xj, ROT_DIM // 2, 1) * s2
            if j < n_q:
                q_ref[rows, j * LANES:(j + 1) * LANES] = (rot * ATT_Q_SCALE).astype(BF16)
            else:
                jk = j - n_q
                k_ref[rows, jk * LANES:(jk + 1) * LANES] = rot.astype(BF16)
        v_ref[rows, :] = qkv[:, (n_q + n_k) * LANES:].astype(BF16)


def _attn_proj(x, g, w, tabs, seq_len):
    n = x.shape[0]
    tm = TOKEN_TILE
    per_seq = seq_len // tm
    row = pl.BlockSpec((tm, D_MODEL), lambda i: (i, 0))
    tab = pl.BlockSpec((tm, LANES), lambda i: (i % per_seq, 0))
    kv_w = N_KV_HEADS * LANES
    return pl.pallas_call(
        _attn_proj_kernel,
        grid=(n // tm,),
        in_specs=[row, _const_spec((1, D_MODEL)), _const_spec(w.shape), tab, tab, tab],
        out_specs=[row, pl.BlockSpec((tm, kv_w), lambda i: (i, 0)), pl.BlockSpec((tm, kv_w), lambda i: (i, 0))],
        out_shape=[jax.ShapeDtypeStruct((n, D_MODEL), BF16), jax.ShapeDtypeStruct((n, kv_w), BF16),
                   jax.ShapeDtypeStruct((n, kv_w), BF16)],
        compiler_params=_params(("parallel",)),
        name="attn_proj",
    )(x, g, w, *tabs)


def _attn_core_kernel(sink_ref, q_ref, kp_ref, km_ref, kn_ref, vp_ref, vm_ref, vn_ref, x_ref, g_ref, wo_ref,
                      o_ref, kext_ref, vt_ref, obuf_ref, bias_ref, *, n_blocks):
    tq = q_ref.shape[0]
    blocks = tq // ATT_BLOCK
    i = pl.program_id(1)
    blk = ATT_BLOCK
    kext_ref[0:blk] = kp_ref[...]
    kext_ref[blk:blk + tq] = km_ref[...]
    kext_ref[blk + tq:] = kn_ref[...]
    vt_ref[:, 0:blk] = vp_ref[...].T
    vt_ref[:, blk:blk + tq] = vm_ref[...].T
    vt_ref[:, blk + tq:] = vn_ref[...].T

    heads_per_col = LANES // HEAD_DIM
    width = heads_per_col * blk
    n_cols = N_HEADS // heads_per_col
    key = lax.broadcasted_iota(jnp.int32, (blk, width), 0)
    qlane = lax.broadcasted_iota(jnp.int32, (blk, width), 1)
    qpos = qlane & (blk - 1)
    neg_inf = jnp.float32(-jnp.inf)
    bias_ref[0] = jnp.where(key >= qpos, 0.0, neg_inf)
    bias_ref[1] = jnp.where(key <= qpos, 0.0, neg_inf)
    lane = lax.broadcasted_iota(jnp.int32, (blk, LANES), 1)
    lo = lane < HEAD_DIM
    top = lax.broadcasted_iota(jnp.int32, (LANES, blk), 0) < HEAD_DIM
    first_head = lax.broadcasted_iota(jnp.int32, (1, width), 1) < blk
    sink_rows = [jnp.where(first_head, sink_ref[heads_per_col * c], sink_ref[heads_per_col * c + 1])
                 for c in range(n_cols)]

    def scores(qb, c):
        r0 = qb * blk
        g = c * heads_per_col // GROUP
        kg = kext_ref[r0:r0 + 3 * blk, g * LANES:(g + 1) * LANES]
        qp = q_ref[r0:r0 + blk, c * LANES:(c + 1) * LANES]
        zero = jnp.zeros_like(qp)
        qs = jnp.concatenate([jnp.where(lo, qp, zero), jnp.where(lo, zero, qp)], axis=0)
        return lax.dot_general(kg, qs, (((1,), (1,)), ((), ())), preferred_element_type=F32)

    def finish(qb, c, st):
        gb = i * blocks + qb
        r0 = qb * blk
        g = c * heads_per_col // GROUP
        s_prev = st[:blk] + (bias_ref[0] + jnp.where(gb == 0, neg_inf, 0.0))
        s_mid = st[blk:2 * blk]
        s_next = st[2 * blk:] + (bias_ref[1] + jnp.where(gb == n_blocks - 1, neg_inf, 0.0))
        m = jnp.max(jnp.maximum(jnp.maximum(s_prev, s_mid), s_next), axis=0, keepdims=True)
        m = jnp.maximum(m, sink_rows[c])
        p_prev, p_mid, p_next = jnp.exp2(s_prev - m), jnp.exp2(s_mid - m), jnp.exp2(s_next - m)
        denom = jnp.sum(p_prev + p_mid + p_next, axis=0, keepdims=True) + jnp.exp2(sink_rows[c] - m)
        pt = jnp.concatenate([p_prev, p_mid, p_next], axis=0).astype(BF16)
        vt = vt_ref[g * LANES:(g + 1) * LANES, r0:r0 + 3 * blk]
        ot = jnp.dot(vt, pt, preferred_element_type=F32) * (1.0 / denom)
        obuf_ref[r0:r0 + blk, c * LANES:(c + 1) * LANES] = jnp.where(top, ot[:, :blk], ot[:, blk:]).T.astype(BF16)

    tiles = [(qb, c) for qb in range(blocks) for c in range(n_cols)]
    pending = [scores(*tile) for tile in tiles[:ATT_LOOKAHEAD]]
    for t, (qb, c) in enumerate(tiles):
        if t + ATT_LOOKAHEAD < len(tiles):
            pending.append(scores(*tiles[t + ATT_LOOKAHEAD]))
        finish(qb, c, pending.pop(0))

    y = jnp.dot(obuf_ref[...], wo_ref[...], preferred_element_type=F32)
    o_ref[...] = x_ref[...] + _rms(y, g_ref[...])


def _attn_core(sink, q, kd, vd, x, g, wo, batch, seq_len):
    tq = ATT_Q_TILE
    blocks = tq // ATT_BLOCK
    n_blocks = seq_len // ATT_BLOCK
    kv_w = N_KV_HEADS * LANES
    q3 = q.reshape(batch, seq_len, D_MODEL)
    k3 = kd.reshape(batch, seq_len, kv_w)
    v3 = vd.reshape(batch, seq_len, kv_w)
    x3 = x.reshape(batch, seq_len, D_MODEL)
    row = pl.BlockSpec((None, tq, D_MODEL), lambda b, i: (b, i, 0))
    kv_main = pl.BlockSpec((None, tq, kv_w), lambda b, i: (b, i, 0))
    kv_prev = pl.BlockSpec((None, ATT_BLOCK, kv_w), lambda b, i: (b, jnp.maximum(i * blocks - 1, 0), 0))
    kv_next = pl.BlockSpec((None, ATT_BLOCK, kv_w), lambda b, i: (b, jnp.minimum((i + 1) * blocks, n_blocks - 1), 0))
    out = pl.pallas_call(
        functools.partial(_attn_core_kernel, n_blocks=n_blocks),
        grid=(batch, seq_len // tq),
        in_specs=[pl.BlockSpec(memory_space=pltpu.SMEM), row, kv_prev, kv_main, kv_next, kv_prev, kv_main, kv_next,
                  row, _const_spec((1, D_MODEL)), _const_spec(wo.shape)],
        out_specs=row,
        out_shape=jax.ShapeDtypeStruct((batch, seq_len, D_MODEL), F32),
        scratch_shapes=[pltpu.VMEM((tq + 2 * ATT_BLOCK, kv_w), BF16), pltpu.VMEM((kv_w, tq + 2 * ATT_BLOCK), BF16),
                        pltpu.VMEM((tq, D_MODEL), BF16),
                        pltpu.VMEM((2, ATT_BLOCK, LANES // HEAD_DIM * ATT_BLOCK), F32)],
        compiler_params=_params(("parallel", "parallel")),
        name="attn_core",
    )(sink, q3, k3, k3, k3, v3, v3, v3, x3, g, wo)
    return out.reshape(batch * seq_len, D_MODEL)


def _ret_proj_kernel(x_ref, g_ref, w_ref, cos_ref, sin_ref, o_ref, gate_ref):
    half = RET_QK_DIM // 2
    n_chunks = RET_IN_DIM // D_MODEL
    sub = x_ref.shape[0] // PROJ_SUB_BLOCKS
    for s in range(PROJ_SUB_BLOCKS):
        rows = slice(s * sub, (s + 1) * sub)
        h = _rms(x_ref[rows, :], g_ref[...]).astype(BF16)
        cos, sin = cos_ref[rows, :], sin_ref[rows, :]
        for c in range(n_chunks):
            y = jnp.dot(h, w_ref[:, c * D_MODEL:(c + 1) * D_MODEL], preferred_element_type=F32)
            if c < 2:
                scale = 1.0 if c == 0 else RET_QK_DIM ** -0.5
                for hd in range(RET_HEADS):
                    x1 = y[:, hd * RET_QK_DIM:hd * RET_QK_DIM + half]
                    x2 = y[:, hd * RET_QK_DIM + half:(hd + 1) * RET_QK_DIM]
                    base = c * D_MODEL + hd * RET_QK_DIM
                    o_ref[rows, base:base + half] = ((x1 * cos - x2 * sin) * scale).astype(BF16)
                    o_ref[rows, base + half:base + 2 * half] = ((x2 * cos + x1 * sin) * scale).astype(BF16)
            elif c < 4:
                o_ref[rows, c * D_MODEL:(c + 1) * D_MODEL] = y.astype(BF16)
            else:
                gate_ref[rows, (c - 4) * D_MODEL:(c - 3) * D_MODEL] = (y * jax.nn.sigmoid(y)).astype(BF16)


def _ret_proj(x, g, w, tabs, seq_len):
    n = x.shape[0]
    tm = RET_PROJ_TILE
    per_seq = seq_len // tm
    row = pl.BlockSpec((tm, D_MODEL), lambda i: (i, 0))
    tab = pl.BlockSpec((tm, LANES), lambda i: (i % per_seq, 0))
    return pl.pallas_call(
        _ret_proj_kernel,
        grid=(n // tm,),
        in_specs=[row, _const_spec((1, D_MODEL)), _const_spec(w.shape), tab, tab],
        out_specs=[pl.BlockSpec((tm, RET_QKV_DIM), lambda i: (i, 0)), pl.BlockSpec((tm, RET_V_TOTAL), lambda i: (i, 0))],
        out_shape=[jax.ShapeDtypeStruct((n, RET_QKV_DIM), BF16), jax.ShapeDtypeStruct((n, RET_V_TOTAL), BF16)],
        compiler_params=_params(("parallel",)),
        name="ret_proj",
    )(x, g, w, *tabs)


def _ret_core_kernel(lgf_ref, lgb_ref, q_ref, k_ref, v_ref, o_ref, acc_ref, st_ref, dmat_ref):
    blk = RET_BLOCK
    seq_len = q_ref.shape[0]
    n_chunks = seq_len // blk
    hd = pl.program_id(1)
    lgf = lgf_ref[hd]
    lgb = lgb_ref[hd]
    ii = lax.broadcasted_iota(jnp.int32, (blk, blk), 0)
    jj = lax.broadcasted_iota(jnp.int32, (blk, blk), 1)
    diff = (ii - jj).astype(F32)
    dmat_ref[...] = jnp.where(diff >= 0, jnp.exp(lgf * jnp.maximum(diff, 0.0)),
                              jnp.exp(lgb * jnp.maximum(-diff, 0.0)))
    idx = lax.broadcasted_iota(jnp.int32, (blk, 1), 0).astype(F32)
    qdec_f = jnp.exp(lgf * (idx + 1.0))
    kdec_f = jnp.exp(lgf * (blk - 1.0 - idx))
    cdec_f = jnp.exp(lgf * blk)
    qdec_b = jnp.exp(lgb * (blk - idx))
    kdec_b = jnp.exp(lgb * idx)
    cdec_b = jnp.exp(lgb * blk)

    def state_delta(kc, vc, kdec):
        kd = (kc.astype(F32) * kdec).astype(BF16)
        return lax.dot_general(kd, vc, (((0,), (0,)), ((), ())), preferred_element_type=F32)

    st_ref[...] = jnp.zeros_like(st_ref)

    def fwd(c, carry):
        rows = pl.ds(pl.multiple_of(c * blk, blk), blk)
        qc, kc, vc = q_ref[rows, :], k_ref[rows, :], v_ref[rows, :]
        s = lax.dot_general(qc, kc, (((1,), (1,)), ((), ())), preferred_element_type=F32)
        delta = state_delta(kc, vc, kdec_f)
        cross = jnp.dot(qc, st_ref[...].astype(BF16), preferred_element_type=F32) * qdec_f
        intra = jnp.dot((s * dmat_ref[...]).astype(BF16), vc, preferred_element_type=F32)
        acc_ref[rows, :] = intra + cross
        st_ref[...] = cdec_f * st_ref[...] + delta
        return carry

    lax.fori_loop(0, n_chunks, fwd, 0, unroll=min(RET_UNROLL, n_chunks))

    st_ref[...] = jnp.zeros_like(st_ref)

    def bwd(t, carry):
        c = n_chunks - 1 - t
        rows = pl.ds(pl.multiple_of(c * blk, blk), blk)
        qc, kc, vc = q_ref[rows, :], k_ref[rows, :], v_ref[rows, :]
        delta = state_delta(kc, vc, kdec_b)
        cross = jnp.dot(qc, st_ref[...].astype(BF16), preferred_element_type=F32) * qdec_b
        o_ref[rows, :] = (acc_ref[rows, :] + cross).astype(BF16)
        st_ref[...] = cdec_b * st_ref[...] + delta
        return carry

    lax.fori_loop(0, n_chunks, bwd, 0, unroll=min(RET_UNROLL, n_chunks))


def _ret_core(lgf, lgb, proj, batch, seq_len):
    p3 = proj.reshape(batch, seq_len, RET_QKV_DIM)
    qk_blocks = D_MODEL // RET_QK_DIM
    v_off = 2 * D_MODEL // RET_V_DIM
    smem = pl.BlockSpec(memory_space=pltpu.SMEM)
    out = pl.pallas_call(
        _ret_core_kernel,
        grid=(batch, RET_HEADS),
        in_specs=[smem, smem,
                  pl.BlockSpec((None, seq_len, RET_QK_DIM), lambda b, h: (b, 0, h)),
                  pl.BlockSpec((None, seq_len, RET_QK_DIM), lambda b, h: (b, 0, qk_blocks + h)),
                  pl.BlockSpec((None, seq_len, RET_V_DIM), lambda b, h: (b, 0, v_off + h))],
        out_specs=pl.BlockSpec((None, seq_len, RET_V_DIM), lambda b, h: (b, 0, h)),
        out_shape=jax.ShapeDtypeStruct((batch, seq_len, RET_V_TOTAL), BF16),
        scratch_shapes=[pltpu.VMEM((seq_len, RET_V_DIM), F32), pltpu.VMEM((RET_QK_DIM, RET_V_DIM), F32),
                        pltpu.VMEM((RET_BLOCK, RET_BLOCK), F32)],
        compiler_params=_params(("parallel", "parallel")),
        name="ret_core",
    )(lgf, lgb, p3, p3, p3)
    return out.reshape(batch * seq_len, RET_V_TOTAL)


def _ret_out_kernel(y_ref, gate_ref, x_ref, g_ref, w_ref, o_ref, z_ref):
    sub = x_ref.shape[0] // PROJ_SUB_BLOCKS
    for s in range(PROJ_SUB_BLOCKS):
        rows = slice(s * sub, (s + 1) * sub)
        for hd in range(RET_HEADS):
            cols = slice(hd * RET_V_DIM, (hd + 1) * RET_V_DIM)
            y = y_ref[rows, cols].astype(F32)
            yn = y * lax.rsqrt(jnp.mean(y * y, axis=-1, keepdims=True) + NORM_EPS)
            z_ref[rows, cols] = (gate_ref[rows, cols].astype(F32) * yn).astype(BF16)
        out = jnp.dot(z_ref[rows, :], w_ref[...], preferred_element_type=F32)
        o_ref[rows, :] = x_ref[rows, :] + _rms(out, g_ref[...])


def _ret_out(y, gate, x, g, w):
    n = x.shape[0]
    tm = TOKEN_TILE
    row = pl.BlockSpec((tm, D_MODEL), lambda i: (i, 0))
    wide = pl.BlockSpec((tm, RET_V_TOTAL), lambda i: (i, 0))
    return pl.pallas_call(
        _ret_out_kernel,
        grid=(n // tm,),
        in_specs=[wide, wide, row, _const_spec((1, D_MODEL)), _const_spec(w.shape)],
        out_specs=row,
        out_shape=jax.ShapeDtypeStruct((n, D_MODEL), F32),
        scratch_shapes=[pltpu.VMEM((tm, RET_V_TOTAL), BF16)],
        compiler_params=_params(("parallel",)),
        name="ret_out",
    )(y, gate, x, g, w)


def _angles(seq_len, half, theta):
    inv_freq = theta ** (-np.arange(half, dtype=np.float64) / half)
    return np.arange(seq_len, dtype=np.float64)[:, None] * inv_freq[None, :]


def _attn_tables(seq_len):
    half = ROT_DIM // 2
    ang = _angles(seq_len, half, ROPE_THETA)
    cos, sin = np.cos(ang), np.sin(ang)
    pad = HEAD_DIM - ROT_DIM
    ones = np.ones((seq_len, pad))
    zeros = np.zeros((seq_len, pad))
    zh = np.zeros((seq_len, half))
    c = np.concatenate([cos, cos, ones], axis=1)
    s1 = np.concatenate([-sin, zh, zeros], axis=1)
    s2 = np.concatenate([zh, sin, zeros], axis=1)
    reps = LANES // HEAD_DIM
    return tuple(jnp.asarray(np.tile(t, (1, reps)), F32) for t in (c, s1, s2))


def _ret_tables(seq_len):
    ang = _angles(seq_len, RET_QK_DIM // 2, RET_THETA)
    return jnp.asarray(np.cos(ang), F32), jnp.asarray(np.sin(ang), F32)


def _attn_weights(w_qkv):
    q_dim = N_HEADS * HEAD_DIM
    kv_dim = N_KV_HEADS * HEAD_DIM
    wq = w_qkv[:, :q_dim]
    wk = w_qkv[:, q_dim:q_dim + kv_dim].reshape(D_MODEL, N_KV_HEADS, 1, HEAD_DIM)
    wv = w_qkv[:, q_dim + kv_dim:].reshape(D_MODEL, N_KV_HEADS, 1, HEAD_DIM)
    reps = LANES // HEAD_DIM
    wk = jnp.broadcast_to(wk, (D_MODEL, N_KV_HEADS, reps, HEAD_DIM)).reshape(D_MODEL, N_KV_HEADS * LANES)
    wv = jnp.broadcast_to(wv, (D_MODEL, N_KV_HEADS, reps, HEAD_DIM)).reshape(D_MODEL, N_KV_HEADS * LANES)
    return jnp.concatenate([wq, wk, wv], axis=1).astype(BF16)


def _trunk(x, gains, ffn_w, attn_w, attn_wo, sink, ret_w, ret_wo, lgf, lgb, attn_tabs, ret_tabs):
    batch, seq_len, _ = x.shape
    x = x.reshape(batch * seq_len, D_MODEL)
    for layer in range(2):
        g = gains[layer]
        x = _ffn(x, g[0], g[1], *ffn_w, layer, 0)
        if layer == 0:
            q, kd, vd = _attn_proj(x, g[2], attn_w, attn_tabs, seq_len)
            x = _attn_core(sink, q, kd, vd, x, g[3], attn_wo, batch, seq_len)
        else:
            proj, gate = _ret_proj(x, g[2], ret_w, ret_tabs, seq_len)
            y = _ret_core(lgf, lgb, proj, batch, seq_len)
            x = _ret_out(y, gate, x, g[3], ret_wo)
        x = _ffn(x, g[4], g[5], *ffn_w, layer, 1)
    return x.reshape(batch, seq_len, D_MODEL)


def kernel(x_prompt, x_sample, norm_gains, ffn_w_in, ffn_w_out, attn_w_qkv, attn_w_o, attn_sink, ret_w_in, ret_w_o, ret_decay_fwd, ret_decay_bwd):
    gains = norm_gains.astype(F32).reshape(2, 6, 1, D_MODEL)
    ffn_w = (ffn_w_in.astype(BF16), ffn_w_out.astype(BF16))
    attn_w = _attn_weights(attn_w_qkv[0])
    attn_wo = attn_w_o[0].astype(BF16)
    sink = attn_sink[0].astype(F32) * LOG2_E
    ret_w = ret_w_in[0].astype(BF16)
    ret_wo = ret_w_o[0].astype(BF16)
    lgf = jax.nn.log_sigmoid(ret_decay_fwd[0].astype(F32))
    lgb = jax.nn.log_sigmoid(ret_decay_bwd[0].astype(F32))
    outs = []
    for x in (x_prompt, x_sample):
        seq_len = x.shape[1]
        outs.append(_trunk(x, gains, ffn_w, attn_w, attn_wo, sink, ret_w, ret_wo, lgf, lgb,
                           _attn_tables(seq_len), _ret_tables(seq_len)))
    return tuple(outs)
```
